```python
import math, functools
import jax, jax.numpy as jnp
from jax import lax
import numpy as np

D_MODEL = 1024
BATCH = 2
SEQ = 16384
DEPTH = 1
DEC_BATCH = 128
DEC_SEQ = 4
PAST_LEN = 8192
PAGE_SIZE = 128

N_HEADS_A = 8
HEAD_DIM = 64
D_ATTN = N_HEADS_A * HEAD_DIM
Q_BLOCK = 128
ATTN_SCALE = HEAD_DIM ** -0.5
FORGET_BIAS_INIT = 3.0
D_SSM = 512
SSM_GROUP = 16
N_GROUPS = D_SSM // SSM_GROUP
STATE_DIM = 64
DT_MIN = 0.001
DT_MAX = 0.1
PEER_HEADS = 8
N_KEYS = 128
N_EXPERTS = N_KEYS * N_KEYS
PEER_TOPK = 16
D_KEY = 256
D_HALF = D_KEY // 2
PEER_CHUNK = 256
D_IN = 3 * D_ATTN + N_HEADS_A + D_SSM + 2 * D_MODEL
NORM_EPS = 1e-6
NEG_INF = -1e30

kernel_name = "fox_s5_peer_hybrid_step"


def rmsnorm(x, g):
    xf = x.astype(jnp.float32)
    r = lax.rsqrt(jnp.mean(xf * xf, axis=-1, keepdims=True) + NORM_EPS)
    return (xf * r).astype(x.dtype) * g.astype(x.dtype)


def fox_attend_prompt(q, k, v, logf):
    n, s = q.shape[:2]
    nb = s // Q_BLOCK
    c_k = jnp.swapaxes(jnp.cumsum(logf, axis=1), 1, 2)
    kf = k.astype(jnp.float32)
    vf = v.astype(jnp.float32)
    k_pos = jnp.arange(s)
    q_blocks = jnp.moveaxis(q.reshape(n, nb, Q_BLOCK, N_HEADS_A, HEAD_DIM), 1, 0)
    c_blocks = jnp.moveaxis(c_k.reshape(n, N_HEADS_A, nb, Q_BLOCK), 2, 0)

    def one_block(args):
        i, qb, cb = args
        sc = jnp.einsum('nqhd,nkhd->nhqk', qb.astype(jnp.float32), kf) * ATTN_SCALE
        sc = sc + cb[..., :, None] - c_k[..., None, :]
        q_pos = i * Q_BLOCK + jnp.arange(Q_BLOCK)
        mask = k_pos[None, :] <= q_pos[:, None]
        p = jax.nn.softmax(jnp.where(mask, sc, NEG_INF), axis=-1)
        return jnp.einsum('nhqk,nkhd->nqhd', p, vf)

    o = lax.map(one_block, (jnp.arange(nb), q_blocks, c_blocks))
    return jnp.moveaxis(o, 0, 1).reshape(n, s, N_HEADS_A, HEAD_DIM).astype(q.dtype)


def fox_attend_sample(q, k, v, logf, cache_k, cache_v, cache_logf, page_table, layer):
    n_past = page_table.shape[1] * PAGE_SIZE
    t = q.shape[1]
    k_pos = jnp.arange(n_past + t)
    q_pos = n_past + jnp.arange(t)
    mask = k_pos[None, :] <= q_pos[:, None]

    def one_seq(args):
        qs, ks, vs, ls, pages = args
        kp = cache_k[layer, pages].reshape(n_past, N_HEADS_A, HEAD_DIM)
        vp = cache_v[layer, pages].reshape(n_past, N_HEADS_A, HEAD_DIM)
        lp = cache_logf[layer, pages].reshape(n_past, N_HEADS_A)
        kall = jnp.concatenate([kp.astype(jnp.float32), ks.astype(jnp.float32)], axis=0)
        vall = jnp.concatenate([vp.astype(jnp.float32), vs.astype(jnp.float32)], axis=0)
        c = jnp.cumsum(jnp.concatenate([lp.astype(jnp.float32), ls], axis=0), axis=0).T
        sc = jnp.einsum('qhd,khd->hqk', qs.astype(jnp.float32), kall) * ATTN_SCALE
        sc = sc + c[:, n_past:, None] - c[:, None, :]
        p = jax.nn.softmax(jnp.where(mask, sc, NEG_INF), axis=-1)
        return jnp.einsum('hqk,khd->qhd', p, vall)

    o = lax.map(one_seq, (q, k, v, logf, page_table))
    return o.astype(q.dtype)


def s5_discretize(lam_re, lam_im, log_dt, b_re, b_im):
    lam = lax.complex(lam_re.astype(jnp.float32), lam_im.astype(jnp.float32))
    dt = jnp.exp(log_dt.astype(jnp.float32))[:, None]
    lam_bar = jnp.exp(lam * dt)
    b = lax.complex(b_re.astype(jnp.float32), b_im.astype(jnp.float32))
    b_bar = ((lam_bar - 1.0) / lam)[:, :, None] * b
    return lam_bar, b_bar


def s5_mix(u, h0, lam_bar, b_bar, c_re, c_im, d_skip, w_glu):
    n, s = u.shape[:2]
    ug = u.astype(jnp.float32).reshape(n, s, N_GROUPS, SSM_GROUP)
    bu = jnp.einsum('gpc,nsgc->nsgp', b_bar, ug)
    bu = bu.at[:, 0].add(lam_bar * h0)
    a = jnp.broadcast_to(lam_bar, bu.shape)

    def combine(e1, e2):
        a1, b1 = e1
        a2, b2 = e2
        return a1 * a2, a2 * b1 + b2

    _, h = lax.associative_scan(combine, (a, bu), axis=1)
    c = lax.complex(c_re.astype(jnp.float32), c_im.astype(jnp.float32))
    y = jnp.real(jnp.einsum('gcp,nsgp->nsgc', c, h)) + d_skip.astype(jnp.float32) * ug
    z = jax.nn.gelu(y.reshape(n, s, D_SSM))
    out = z * jax.nn.sigmoid(z @ w_glu.astype(jnp.float32))
    return out.astype(u.dtype), h[:, -1]


def peer(xn, w_query, sub_keys, expert_u, expert_v):
    lead = xn.shape[:-1]
    xf = xn.reshape(-1, D_MODEL)
    n = xf.shape[0]
    n_pad = -(-n // PEER_CHUNK) * PEER_CHUNK
    xc = jnp.pad(xf, ((0, n_pad - n), (0, 0))).reshape(n_pad // PEER_CHUNK, PEER_CHUNK, D_MODEL)
    keys = sub_keys.astype(jnp.float32)

    def chunk(xb):
        q = (xb @ w_query).astype(jnp.float32).reshape(PEER_CHUNK, PEER_HEADS, 2, D_HALF)
        s = jnp.einsum('nhtd,htkd->nhtk', q, keys)
        sv, si = lax.top_k(s, PEER_TOPK)
        comb = sv[:, :, 0, :, None] + sv[:, :, 1, None, :]
        cv, ci = lax.top_k(comb.reshape(PEER_CHUNK, PEER_HEADS, PEER_TOPK * PEER_TOPK), PEER_TOPK)
        i1 = jnp.take_along_axis(si[:, :, 0], ci // PEER_TOPK, axis=-1)
        i2 = jnp.take_along_axis(si[:, :, 1], ci % PEER_TOPK, axis=-1)
        idx = i1 * N_KEYS + i2
        w = jax.nn.softmax(cv, axis=-1)
        hidden = jax.nn.gelu(jnp.einsum('nd,nhkd->nhk', xb, expert_u[idx]))
        return jnp.einsum('nhk,nhkd->nd', (w * hidden).astype(xb.dtype), expert_v[idx])

    out = lax.map(chunk, xc).reshape(n_pad, D_MODEL)[:n]
    return out.reshape(*lead, D_MODEL).astype(xn.dtype)


def hybrid_layer(x, attend, h0, p):
    lead = x.shape[:-1]
    xn = rmsnorm(x, p['norm1_g'])
    z = xn @ p['w_in']
    o1 = D_ATTN
    o2 = 2 * D_ATTN
    o3 = 3 * D_ATTN
    o4 = o3 + N_HEADS_A
    o5 = o4 + D_SSM
    o6 = o5 + D_MODEL
    q = rmsnorm(z[..., :o1].reshape(*lead, N_HEADS_A, HEAD_DIM), p['q_norm_g'])
    k = rmsnorm(z[..., o1:o2].reshape(*lead, N_HEADS_A, HEAD_DIM), p['k_norm_g'])
    v = z[..., o2:o3].reshape(*lead, N_HEADS_A, HEAD_DIM)
    logf = jax.nn.log_sigmoid(z[..., o3:o4].astype(jnp.float32) + p['b_forget'].astype(jnp.float32))
    u = z[..., o4:o5]
    gate_attn = jax.nn.sigmoid(z[..., o5:o6])
    gate_ssm = jax.nn.sigmoid(z[..., o6:])
    o_attn = attend(q, k, v, logf).reshape(*lead, D_ATTN)
    lam_bar, b_bar = s5_discretize(p['lam_re'], p['lam_im'], p['log_dt'], p['b_re'], p['b_im'])
    o_ssm, h_last = s5_mix(u, h0, lam_bar, b_bar, p['c_re'], p['c_im'], p['d_skip'], p['w_glu'])
    mixed = gate_attn * (o_attn @ p['w_proj_attn']) + gate_ssm * (o_ssm @ p['w_proj_ssm'])
    x = x + mixed @ p['w_out']
    x = x + peer(rmsnorm(x, p['norm2_g']), p['w_query'], p['sub_keys'], p['expert_u'], p['expert_v'])
    return x, k, v, logf, h_last


def setup_inputs(seed: int = 0) -> dict:
    key = jax.random.key(seed)
    ks = jax.random.split(key, 32)
    f32 = jnp.float32
    n_pages = PAST_LEN // PAGE_SIZE
    n_used = DEC_BATCH * n_pages
    n_pool = n_used + n_used // 4

    def nrm(k, shape, s):
        return jax.random.normal(k, shape, f32) * s

    x_prompt = nrm(ks[0], (BATCH, SEQ, D_MODEL), 1.0)
    x_sample = nrm(ks[1], (DEC_BATCH, DEC_SEQ, D_MODEL), 1.0)
    cache_k = nrm(ks[2], (DEPTH, n_pool, PAGE_SIZE, N_HEADS_A, HEAD_DIM), 1.0)
    cache_v = nrm(ks[3], (DEPTH, n_pool, PAGE_SIZE, N_HEADS_A, HEAD_DIM), 1.0)
    cache_logf = jax.nn.log_sigmoid(FORGET_BIAS_INIT + nrm(ks[4], (DEPTH, n_pool, PAGE_SIZE, N_HEADS_A), 1.0))
    state_ssm_re = nrm(ks[5], (DEPTH, DEC_BATCH, N_GROUPS, STATE_DIM), 0.1)
    state_ssm_im = nrm(ks[6], (DEPTH, DEC_BATCH, N_GROUPS, STATE_DIM), 0.1)
    page_table = jax.random.permutation(ks[7], n_pool)[:n_used].reshape(DEC_BATCH, n_pages).astype(jnp.int32)

    norm1_g = 1.0 + nrm(ks[8], (DEPTH, D_MODEL), 0.01)
    w_in = nrm(ks[9], (DEPTH, D_MODEL, D_IN), D_MODEL ** -0.5)
    b_forget = FORGET_BIAS_INIT + nrm(ks[10], (DEPTH, N_HEADS_A), 0.1)
    q_norm_g = 1.0 + nrm(ks[11], (DEPTH, HEAD_DIM), 0.01)
    k_norm_g = 1.0 + nrm(ks[12], (DEPTH, HEAD_DIM), 0.01)
    ssm_lam_re = -0.5 + nrm(ks[13], (DEPTH, N_GROUPS, STATE_DIM), 0.01)
    ssm_lam_im = math.pi * jnp.arange(STATE_DIM, dtype=f32)[None, None, :] + nrm(ks[14], (DEPTH, N_GROUPS, STATE_DIM), 0.01)
    ssm_log_dt = jax.random.uniform(ks[15], (DEPTH, N_GROUPS), f32, math.log(DT_MIN), math.log(DT_MAX))
    ssm_b_re = nrm(ks[16], (DEPTH, N_GROUPS, STATE_DIM, SSM_GROUP), (2 * SSM_GROUP) ** -0.5)
    ssm_b_im = nrm(ks[17], (DEPTH, N_GROUPS, STATE_DIM, SSM_GROUP), (2 * SSM_GROUP) ** -0.5)
    ssm_c_re = nrm(ks[18], (DEPTH, N_GROUPS, SSM_GROUP, STATE_DIM), (2 * STATE_DIM) ** -0.5)
    ssm_c_im = nrm(ks[19], (DEPTH, N_GROUPS, SSM_GROUP, STATE_DIM), (2 * STATE_DIM) ** -0.5)
    ssm_d = nrm(ks[20], (DEPTH, N_GROUPS, SSM_GROUP), 1.0)
    w_glu = nrm(ks[21], (DEPTH, D_SSM, D_SSM), D_SSM ** -0.5)
    w_proj_attn = nrm(ks[22], (DEPTH, D_ATTN, D_MODEL), D_ATTN ** -0.5)
    w_proj_ssm = nrm(ks[23], (DEPTH, D_SSM, D_MODEL), D_SSM ** -0.5)
    w_out = nrm(ks[24], (DEPTH, D_MODEL, D_MODEL), D_MODEL ** -0.5)
    norm2_g = 1.0 + nrm(ks[25], (DEPTH, D_MODEL), 0.01)
    w_query = nrm(ks[26], (DEPTH, D_MODEL, PEER_HEADS * D_KEY), D_MODEL ** -0.5)
    sub_keys = nrm(ks[27], (DEPTH, PEER_HEADS, 2, N_KEYS, D_HALF), D_HALF ** -0.5)
    expert_u = nrm(ks[28], (DEPTH, N_EXPERTS, D_MODEL), D_MODEL ** -0.5)
    expert_v = nrm(ks[29], (DEPTH, N_EXPERTS, D_MODEL), PEER_HEADS ** -0.5)
    return {
        "x_prompt": x_prompt, "x_sample": x_sample,
        "cache_k": cache_k, "cache_v": cache_v, "cache_logf": cache_logf,
        "state_ssm_re": state_ssm_re, "state_ssm_im": state_ssm_im,
        "page_table": page_table,
        "norm1_g": norm1_g, "w_in": w_in, "b_forget": b_forget,
        "q_norm_g": q_norm_g, "k_norm_g": k_norm_g,
        "ssm_lam_re": ssm_lam_re, "ssm_lam_im": ssm_lam_im, "ssm_log_dt": ssm_log_dt,
        "ssm_b_re": ssm_b_re, "ssm_b_im": ssm_b_im, "ssm_c_re": ssm_c_re, "ssm_c_im": ssm_c_im,
        "ssm_d": ssm_d, "w_glu": w_glu,
        "w_proj_attn": w_proj_attn, "w_proj_ssm": w_proj_ssm, "w_out": w_out,
        "norm2_g": norm2_g, "w_query": w_query, "sub_keys": sub_keys,
        "expert_u": expert_u, "expert_v": expert_v,
    }


def reference(x_prompt, x_sample, cache_k, cache_v, cache_logf, state_ssm_re, state_ssm_im, page_table,
              norm1_g, w_in, b_forget, q_norm_g, k_norm_g,
              ssm_lam_re, ssm_lam_im, ssm_log_dt, ssm_b_re, ssm_b_im, ssm_c_re, ssm_c_im, ssm_d, w_glu,
              w_proj_attn, w_proj_ssm, w_out, norm2_g, w_query, sub_keys, expert_u, expert_v):
    y_p = x_prompt
    y_s = x_sample
    kp_l, vp_l, lp_l, hrp_l, hip_l = [], [], [], [], []
    ks_l, vs_l, ls_l, hrs_l, his_l = [], [], [], [], []
    for l in range(DEPTH):
        p = {
            'norm1_g': norm1_g[l], 'w_in': w_in[l], 'b_forget': b_forget[l],
            'q_norm_g': q_norm_g[l], 'k_norm_g': k_norm_g[l],
            'lam_re': ssm_lam_re[l], 'lam_im': ssm_lam_im[l], 'log_dt': ssm_log_dt[l],
            'b_re': ssm_b_re[l], 'b_im': ssm_b_im[l], 'c_re': ssm_c_re[l], 'c_im': ssm_c_im[l],
            'd_skip': ssm_d[l], 'w_glu': w_glu[l],
            'w_proj_attn': w_proj_attn[l], 'w_proj_ssm': w_proj_ssm[l], 'w_out': w_out[l],
            'norm2_g': norm2_g[l], 'w_query': w_query[l], 'sub_keys': sub_keys[l],
            'expert_u': expert_u[l], 'expert_v': expert_v[l],
        }
        h0_p = jnp.zeros((y_p.shape[0], N_GROUPS, STATE_DIM), jnp.complex64)
        y_p, k_p, v_p, lf_p, h_p = hybrid_layer(y_p, fox_attend_prompt, h0_p, p)
        h0_s = lax.complex(state_ssm_re[l].astype(jnp.float32), state_ssm_im[l].astype(jnp.float32))
        attend_s = functools.partial(fox_attend_sample, cache_k=cache_k, cache_v=cache_v,
                                     cache_logf=cache_logf, page_table=page_table, layer=l)
        y_s, k_s, v_s, lf_s, h_s = hybrid_layer(y_s, attend_s, h0_s, p)
        kp_l.append(k_p); vp_l.append(v_p); lp_l.append(lf_p)
        hrp_l.append(jnp.real(h_p)); hip_l.append(jnp.imag(h_p))
        ks_l.append(k_s); vs_l.append(v_s); ls_l.append(lf_s)
        hrs_l.append(jnp.real(h_s)); his_l.append(jnp.imag(h_s))
    return (y_p, y_s,
            jnp.stack(kp_l), jnp.stack(vp_l), jnp.stack(lp_l), jnp.stack(hrp_l), jnp.stack(hip_l),
            jnp.stack(ks_l), jnp.stack(vs_l), jnp.stack(ls_l), jnp.stack(hrs_l), jnp.stack(his_l))
```

```python
import functools
import math

import jax
import jax.numpy as jnp
from jax import lax
from jax.experimental import pallas as pl
from jax.experimental.pallas import tpu as pltpu

F32 = jnp.float32
BF16 = jnp.bfloat16

N_HEADS = 8
HEAD_DIM = 64
D_ATTN = N_HEADS * HEAD_DIM
ATTN_SCALE = HEAD_DIM ** -0.5
D_SSM = 512
SSM_GROUP = 16
N_GROUPS = D_SSM // SSM_GROUP
STATE_DIM = 64
N_STATE = N_GROUPS * STATE_DIM
PEER_HEADS = 8
N_KEYS = 128
PEER_TOPK = 16
D_HALF = 128
NORM_EPS = 1e-6
NEG_INF = -1e30
PAGE = 128

LANES = 128
SUBLANES = 8
VMEM_LIMIT = 56 * 1024 * 1024


def _cparams(sem):
    return pltpu.CompilerParams(dimension_semantics=sem, vmem_limit_bytes=VMEM_LIMIT)


def _const_spec(shape):
    nd = len(shape)
    return pl.BlockSpec(shape, lambda *_: (0,) * nd)


def _rms_rows(x, g):
    r = lax.rsqrt(jnp.mean(x * x, axis=-1, keepdims=True) + NORM_EPS)
    return (x * r) * g


def _split_dot(a, b_bf16):
    hi = a.astype(BF16)
    lo = (a - hi.astype(F32)).astype(BF16)
    return (jnp.dot(hi, b_bf16, preferred_element_type=F32)
            + jnp.dot(lo, b_bf16, preferred_element_type=F32))


def _log_sigmoid(x):
    return jnp.minimum(x, 0.0) - jnp.log1p(jnp.exp(-jnp.abs(x)))


def _inproj_kernel(x_ref, g1_ref, w_ref, wf_ref, bf_ref, gq_ref, gk_ref, hsel_ref,
                   qb_ref, kb_ref, vb_ref, kf_ref, vf_ref, lf_ref, u_ref, ga_ref, gs_ref):
    xn = _rms_rows(x_ref[...], g1_ref[...]).astype(BF16)

    def seg(i, n):
        return jnp.dot(xn, w_ref[:, i:i + n], preferred_element_type=F32)

    hsel = hsel_ref[...]

    def head_norm(z, g):
        ms = _split_dot(z * z, hsel)
        return (z * lax.rsqrt(ms + NORM_EPS)) * g

    q = head_norm(seg(0, D_ATTN), gq_ref[...])
    qb_ref[...] = (q * ATTN_SCALE).astype(BF16)
    k = head_norm(seg(D_ATTN, D_ATTN), gk_ref[...])
    kf_ref[...] = k
    kb_ref[...] = k.astype(BF16)
    v = seg(2 * D_ATTN, D_ATTN)
    vf_ref[...] = v
    vb_ref[...] = v.astype(BF16)
    u_ref[...] = seg(3 * D_ATTN, D_SSM)
    o = 3 * D_ATTN + D_SSM
    d = ga_ref.shape[1]
    ga_ref[...] = jax.nn.sigmoid(seg(o, d)).astype(BF16)
    gs_ref[...] = jax.nn.sigmoid(seg(o + d, d)).astype(BF16)
    zf = jnp.dot(xn, wf_ref[...], preferred_element_type=F32) + bf_ref[...]
    lf_ref[...] = _log_sigmoid(zf)[:, :N_HEADS]


def _prep_inproj(norm1_g, w_in, b_forget, q_norm_g, k_norm_g):
    o3 = 3 * D_ATTN
    o4 = o3 + N_HEADS
    wmain = jnp.concatenate([w_in[:, :o3], w_in[:, o4:]], axis=1).astype(BF16)
    wf = jnp.pad(w_in[:, o3:o4], ((0, 0), (0, LANES - N_HEADS))).astype(BF16)
    bfp = jnp.pad(b_forget.astype(F32), (0, LANES - N_HEADS))[None, :]
    gq = jnp.tile(q_norm_g.astype(F32), N_HEADS)[None, :]
    gk = jnp.tile(k_norm_g.astype(F32), N_HEADS)[None, :]
    hsel = jnp.kron(jnp.eye(N_HEADS, dtype=F32),
                    jnp.full((HEAD_DIM, HEAD_DIM), 1.0 / HEAD_DIM, F32)).astype(BF16)
    return norm1_g.astype(F32)[None, :], wmain, wf, bfp, gq, gk, hsel


def _inproj(x, g1, wmain, wf, bfp, gq, gk, hsel, tm):
    n, d = x.shape
    tok = lambda w: pl.BlockSpec((tm, w), lambda i: (i, 0))
    out_shape = (
        jax.ShapeDtypeStruct((n, D_ATTN), BF16), jax.ShapeDtypeStruct((n, D_ATTN), BF16),
        jax.ShapeDtypeStruct((n, D_ATTN), BF16), jax.ShapeDtypeStruct((n, D_ATTN), F32),
        jax.ShapeDtypeStruct((n, D_ATTN), F32), jax.ShapeDtypeStruct((n, N_HEADS), F32),
        jax.ShapeDtypeStruct((n, D_SSM), F32), jax.ShapeDtypeStruct((n, d), BF16),
        jax.ShapeDtypeStruct((n, d), BF16))
    return pl.pallas_call(
        _inproj_kernel, name="inproj",
        grid=(n // tm,),
        in_specs=[tok(d), _const_spec(g1.shape), _const_spec(wmain.shape), _const_spec(wf.shape),
                  _const_spec(bfp.shape), _const_spec(gq.shape), _const_spec(gk.shape),
                  _const_spec(hsel.shape)],
        out_specs=(tok(D_ATTN), tok(D_ATTN), tok(D_ATTN), tok(D_ATTN), tok(D_ATTN), tok(N_HEADS),
                   tok(D_SSM), tok(d), tok(d)),
        out_shape=out_shape,
        compiler_params=_cparams(("arbitrary",)),
    )(x, g1, wmain, wf, bfp, gq, gk, hsel)


def _mix_kernel(oa_ref, os_ref, ga_ref, gs_ref, x_ref, wpa_ref, wps_ref, wo_ref, g2_ref,
                x1_ref, xn_ref):
    pa = jnp.dot(oa_ref[...], wpa_ref[...], preferred_element_type=F32)
    ps = jnp.dot(os_ref[...].astype(BF16), wps_ref[...], preferred_element_type=F32)
    mixed = ga_ref[...].astype(F32) * pa + gs_ref[...].astype(F32) * ps
    x1 = x_ref[...] + jnp.dot(mixed.astype(BF16), wo_ref[...], preferred_element_type=F32)
    x1_ref[...] = x1
    xn_ref[...] = _rms_rows(x1, g2_ref[...]).astype(BF16)


def _mix(oa, os_, ga, gs, x, wpa, wps, wo, g2, tm):
    n, d = x.shape
    tok = lambda w: pl.BlockSpec((tm, w), lambda i: (i, 0))
    return pl.pallas_call(
        _mix_kernel, name="mix",
        grid=(n // tm,),
        in_specs=[tok(D_ATTN), tok(D_SSM), tok(d), tok(d), tok(d), _const_spec(wpa.shape),
                  _const_spec(wps.shape), _const_spec(wo.shape), _const_spec(g2.shape)],
        out_specs=(tok(d), tok(d)),
        out_shape=(jax.ShapeDtypeStruct((n, d), F32), jax.ShapeDtypeStruct((n, d), BF16)),
        compiler_params=_cparams(("arbitrary",)),
    )(oa, os_, ga, gs, x, wpa, wps, wo, g2)


def _split3(a):
    parts = []
    rem = a
    for _ in range(3):
        part = rem.astype(BF16)
        rem = rem - part.astype(F32)
        parts.append(part)
    return parts


def _dot01_right(a, b01):
    return sum(jnp.dot(p, b01, preferred_element_type=F32) for p in _split3(a))


def _dot01_left(a01, b):
    return sum(jnp.dot(a01, p, preferred_element_type=F32) for p in _split3(b))


def _tri_masks(t):
    r = lax.broadcasted_iota(jnp.int32, (t, t), 0)
    c = lax.broadcasted_iota(jnp.int32, (t, t), 1)
    tril = jnp.where(c <= r, 1.0, 0.0).astype(BF16)
    triu = jnp.where(r <= c, 1.0, 0.0).astype(BF16)
    return tril, triu


def _cumsum_kernel(lf_ref, lft_ref, c_ref, ct_ref, crow_ref, ccol_ref):
    t = lf_ref.shape[1]

    @pl.when(pl.program_id(1) == 0)
    def _():
        crow_ref[...] = jnp.zeros_like(crow_ref)
        ccol_ref[...] = jnp.zeros_like(ccol_ref)

    tril, triu = _tri_masks(t)
    cs = crow_ref[...] + _dot01_left(tril, lf_ref[0])
    c_ref[0] = cs
    crow_ref[...] = cs[t - 1:t, :]
    cst = ccol_ref[...] + _dot01_right(lft_ref[0], triu)
    ct_ref[0] = cst
    ccol_ref[...] = cst[:, t - 1:t]


def _cumsum(lf, lft, t):
    n, s, h = lf.shape
    return pl.pallas_call(
        _cumsum_kernel, name="logf_cumsum",
        grid=(n, s // t),
        in_specs=[pl.BlockSpec((1, t, h), lambda b, i: (b, i, 0)),
                  pl.BlockSpec((1, h, t), lambda b, i: (b, 0, i))],
        out_specs=(pl.BlockSpec((1, t, h), lambda b, i: (b, i, 0)),
                   pl.BlockSpec((1, h, t), lambda b, i: (b, 0, i))),
        out_shape=(jax.ShapeDtypeStruct((n, s, h), F32), jax.ShapeDtypeStruct((n, h, s), F32)),
        scratch_shapes=[pltpu.VMEM((1, h), F32), pltpu.VMEM((h, 1), F32)],
        compiler_params=_cparams(("arbitrary", "arbitrary")),
    )(lf, lft)


def _attn_prompt_kernel(q_ref, k_ref, v_ref, c_ref, ct_ref, o_ref, m_ref, l_ref, acc_ref):
    tq = q_ref.shape[1]
    tk = k_ref.shape[1]
    qi = pl.program_id(1)
    ki = pl.program_id(2)

    @pl.when(ki == 0)
    def _():
        m_ref[...] = jnp.full_like(m_ref, NEG_INF)
        l_ref[...] = jnp.zeros_like(l_ref)
        acc_ref[...] = jnp.zeros_like(acc_ref)

    lane_k = lax.broadcasted_iota(jnp.int32, (tk, LANES), 1) < HEAD_DIM
    lane_q = lax.broadcasted_iota(jnp.int32, (tq, LANES), 1) < HEAD_DIM
    keep_lo = jnp.where(lane_k, 1.0, 0.0).astype(BF16)
    keep_hi = jnp.where(lane_k, 0.0, 1.0).astype(BF16)

    def step(diagonal):
        if diagonal:
            row = lax.broadcasted_iota(jnp.int32, (tq, tk), 0)
            col = lax.broadcasted_iota(jnp.int32, (tq, tk), 1)
            causal = col <= row
        for hp in range(N_HEADS // 2):
            sl = slice(hp * LANES, (hp + 1) * LANES)
            qp = q_ref[0, :, sl]
            kp = k_ref[0, :, sl]
            vp = v_ref[0, :, sl]
            alphas, pvs = [], []
            for half in range(2):
                h = 2 * hp + half
                keep = keep_lo if half == 0 else keep_hi
                kh = kp * keep
                vh = vp * keep
                s = lax.dot_general(qp, kh, (((1,), (1,)), ((), ())), preferred_element_type=F32)
                s = s - ct_ref[0, h:h + 1, :]
                if diagonal:
                    s = jnp.where(causal, s, NEG_INF)
                cq = c_ref[0, :, h:h + 1]
                m_prev = m_ref[h]
                m_new = jnp.maximum(m_prev, jnp.max(s, axis=-1, keepdims=True) + cq)
                p = jnp.exp(s - (m_new - cq))
                alpha = jnp.exp(m_prev - m_new)
                l_ref[h] = alpha * l_ref[h] + jnp.sum(p, axis=-1, keepdims=True)
                m_ref[h] = m_new
                alphas.append(alpha)
                pvs.append(jnp.dot(p.astype(BF16), vh, preferred_element_type=F32))
            alpha_pair = jnp.where(lane_q, alphas[0], alphas[1])
            acc_ref[:, sl] = alpha_pair * acc_ref[:, sl] + pvs[0] + pvs[1]

    @pl.when(ki < qi)
    def _():
        step(False)

    @pl.when(ki == qi)
    def _():
        step(True)
        for hp in range(N_HEADS // 2):
            sl = slice(hp * LANES, (hp + 1) * LANES)
            l_pair = jnp.where(lane_q, l_ref[2 * hp], l_ref[2 * hp + 1])
            o_ref[0, :, sl] = (acc_ref[:, sl] / l_pair).astype(o_ref.dtype)


def _attn_prompt(qb, kb, vb, c, ct, t):
    n, s, d = qb.shape
    nb = s // t
    qmap = lambda b, i, j: (b, i, 0)
    kmap = lambda b, i, j: (b, jnp.minimum(i, j), 0)
    return pl.pallas_call(
        _attn_prompt_kernel, name="attn_prompt",
        grid=(n, nb, nb),
        in_specs=[pl.BlockSpec((1, t, d), qmap), pl.BlockSpec((1, t, d), kmap),
                  pl.BlockSpec((1, t, d), kmap), pl.BlockSpec((1, t, N_HEADS), qmap),
                  pl.BlockSpec((1, N_HEADS, t), lambda b, i, j: (b, 0, jnp.minimum(i, j)))],
        out_specs=pl.BlockSpec((1, t, d), qmap),
        out_shape=jax.ShapeDtypeStruct((n, s, d), BF16),
        scratch_shapes=[pltpu.VMEM((N_HEADS, t, 1), F32), pltpu.VMEM((N_HEADS, t, 1), F32),
                        pltpu.VMEM((t, d), F32)],
        compiler_params=_cparams(("arbitrary", "arbitrary", "arbitrary")),
    )(qb, kb, vb, c, ct)


def _attn_paged_kernel(pt_ref, qm_ref, kn_ref, vn_ref, lfn_ref, kc_ref, vc_ref, lfc_ref, e_ref,
                       o_ref, m_ref, l_ref, acc_ref, carry_ref, cq_ref, *, n_new):
    del pt_ref
    p = pl.program_id(1)
    rows = PAGE * N_HEADS
    row = lax.broadcasted_iota(jnp.int32, (rows, LANES), 0)
    col = lax.broadcasted_iota(jnp.int32, (rows, LANES), 1)
    n_cols = n_new * N_HEADS
    same_head = jnp.logical_and(jnp.bitwise_and(row, N_HEADS - 1) == jnp.bitwise_and(col, N_HEADS - 1),
                                col < n_cols)
    krow = lax.broadcasted_iota(jnp.int32, (PAGE, PAGE), 0)
    kcol = lax.broadcasted_iota(jnp.int32, (PAGE, PAGE), 1)
    later = jnp.where(kcol > krow, 1.0, 0.0).astype(BF16)

    def step(k_page, v_page, lf_page, valid):
        kmat = k_page.reshape(rows, HEAD_DIM).astype(BF16)
        vmat = v_page.reshape(rows, HEAD_DIM).astype(BF16)
        s = jnp.dot(kmat, qm_ref[0], preferred_element_type=F32)
        lf_cols = _dot01_right(lf_page, e_ref[...])
        suffix = _dot01_left(later, lf_cols) + carry_ref[...]
        carry_new = suffix[0:1, :] + lf_cols[0:1, :]
        bias = jnp.broadcast_to(suffix[:, None, :], (PAGE, N_HEADS, LANES)).reshape(rows, LANES)
        s = jnp.where(valid, s + bias, NEG_INF)
        return s, vmat, suffix, carry_new

    def accumulate(s, vmat):
        cq = cq_ref[...]
        m_prev = m_ref[...]
        m_new = jnp.maximum(m_prev, jnp.max(s, axis=0, keepdims=True) + cq)
        prob = jnp.exp(s - (m_new - cq))
        alpha = jnp.exp(m_prev - m_new)
        l_ref[...] = alpha * l_ref[...] + jnp.sum(prob, axis=0, keepdims=True)
        m_ref[...] = m_new
        pv = lax.dot_general(vmat, prob.astype(BF16), (((0,), (0,)), ((), ())),
                             preferred_element_type=F32)
        acc_ref[...] = alpha * acc_ref[...] + pv

    @pl.when(p == 0)
    def _():
        carry_ref[...] = jnp.zeros_like(carry_ref)
        key = lax.shift_right_logical(row, 3)
        qry = lax.shift_right_logical(col, 3)
        valid = jnp.logical_and(same_head, jnp.logical_and(key < n_new, key <= qry))
        s, vmat, suffix, carry_new = step(kn_ref[0], vn_ref[0], lfn_ref[0], valid)
        pick = lax.shift_right_logical(kcol, 3) == krow
        cq_ref[...] = -jnp.sum(jnp.where(pick, suffix, 0.0), axis=0, keepdims=True)
        m_ref[...] = jnp.full_like(m_ref, NEG_INF)
        l_ref[...] = jnp.zeros_like(l_ref)
        acc_ref[...] = jnp.zeros_like(acc_ref)
        accumulate(s, vmat)
        carry_ref[...] = carry_new

    @pl.when(p > 0)
    def _():
        s, vmat, _, carry_new = step(kc_ref[0, 0], vc_ref[0, 0], lfc_ref[0, 0], same_head)
        accumulate(s, vmat)
        carry_ref[...] = carry_new

    @pl.when(p == pl.num_programs(1) - 1)
    def _():
        o_ref[0] = acc_ref[...] / l_ref[...]


def _attn_paged(page_table, qm, kn, vn, lfn, cache_k, cache_v, cache_logf, n_new, layer):
    nb, n_pages = page_table.shape
    e = (jnp.arange(LANES)[None, :] % N_HEADS == jnp.arange(N_HEADS)[:, None]).astype(BF16)
    page = lambda b, p, pt: (layer, pt[b, n_pages - jnp.maximum(p, 1)], 0, 0, 0)
    page_lf = lambda b, p, pt: (layer, pt[b, n_pages - jnp.maximum(p, 1)], 0, 0)
    per_seq4 = lambda b, p, pt: (b, 0, 0, 0)
    per_seq3 = lambda b, p, pt: (b, 0, 0)
    grid_spec = pltpu.PrefetchScalarGridSpec(
        num_scalar_prefetch=1,
        grid=(nb, n_pages + 1),
        in_specs=[pl.BlockSpec((1, HEAD_DIM, LANES), per_seq3),
                  pl.BlockSpec((1, PAGE, N_HEADS, HEAD_DIM), per_seq4),
                  pl.BlockSpec((1, PAGE, N_HEADS, HEAD_DIM), per_seq4),
                  pl.BlockSpec((1, PAGE, N_HEADS), per_seq3),
                  pl.BlockSpec((1, 1, PAGE, N_HEADS, HEAD_DIM), page),
                  pl.BlockSpec((1, 1, PAGE, N_HEADS, HEAD_DIM), page),
                  pl.BlockSpec((1, 1, PAGE, N_HEADS), page_lf),
                  pl.BlockSpec((N_HEADS, LANES), lambda b, p, pt: (0, 0))],
        out_specs=pl.BlockSpec((1, HEAD_DIM, LANES), per_seq3),
        scratch_shapes=[pltpu.VMEM((1, LANES), F32), pltpu.VMEM((1, LANES), F32),
                        pltpu.VMEM((HEAD_DIM, LANES), F32), pltpu.VMEM((1, LANES), F32),
                        pltpu.VMEM((1, LANES), F32)])
    return pl.pallas_call(
        functools.partial(_attn_paged_kernel, n_new=n_new), name="attn_paged",
        grid_spec=grid_spec,
        out_shape=jax.ShapeDtypeStruct((nb, HEAD_DIM, LANES), F32),
        compiler_params=_cparams(("arbitrary", "arbitrary")),
    )(page_table, qm, kn, vn, lfn, cache_k, cache_v, cache_logf, e)


S5_COLS = 512


def _s5_prep_kernel(lre_ref, lim_ref, ldt_ref, bre_ref, bim_ref, jp_ref,
                    lbr_ref, lbi_ref, bbr_ref, bbi_ref, pwr_ref, pwi_ref):
    lre = lre_ref[...]
    lim = lim_ref[...]
    dt = jnp.exp(ldt_ref[...])
    mag = jnp.exp(lre * dt)
    ang = lim * dt
    lbr = mag * jnp.cos(ang)
    lbi = mag * jnp.sin(ang)
    lbr_ref[...] = lbr
    lbi_ref[...] = lbi
    nr = lbr - 1.0
    den = lre * lre + lim * lim
    cr = (nr * lre + lbi * lim) / den
    ci = (lbi * lre - nr * lim) / den
    bre = bre_ref[...]
    bim = bim_ref[...]
    bbr_ref[...] = cr * bre - ci * bim
    bbi_ref[...] = cr * bim + ci * bre
    jp = jp_ref[...]
    pmag = jnp.exp(jp * (lre * dt))
    pang = jp * ang
    pwr_ref[...] = pmag * jnp.cos(pang)
    pwi_ref[...] = pmag * jnp.sin(pang)


def _s5_prep(lam_re, lam_im, log_dt, b_re, b_im, chunk):
    lre = lam_re.reshape(1, N_STATE).astype(F32)
    lim = lam_im.reshape(1, N_STATE).astype(F32)
    ldt = jnp.repeat(log_dt.astype(F32), STATE_DIM).reshape(1, N_STATE)
    bre = jnp.transpose(b_re.astype(F32), (2, 0, 1)).reshape(SSM_GROUP, N_STATE)
    bim = jnp.transpose(b_im.astype(F32), (2, 0, 1)).reshape(SSM_GROUP, N_STATE)
    jp = (jnp.arange(chunk, dtype=jnp.int32) // SUBLANES + 1).astype(F32).reshape(chunk, 1)
    vec = jax.ShapeDtypeStruct((1, N_STATE), F32)
    mat = jax.ShapeDtypeStruct((SSM_GROUP, N_STATE), F32)
    tab = jax.ShapeDtypeStruct((chunk, N_STATE), F32)
    return pl.pallas_call(
        _s5_prep_kernel, name="s5_prep",
        out_shape=(vec, vec, mat, mat, tab, tab),
        compiler_params=pltpu.CompilerParams(vmem_limit_bytes=VMEM_LIMIT),
    )(lre, lim, ldt, bre, bim, jp)


def _s5_weights(bbr, bbi, c_re, c_im, d_skip, w_glu):
    eye = jnp.eye(N_GROUPS, dtype=F32)

    def in_map(bb):
        b3 = bb.reshape(SSM_GROUP, N_GROUPS, STATE_DIM)
        return jnp.einsum('cgp,gh->gchp', b3, eye).reshape(D_SSM, N_STATE).astype(BF16)

    def out_map(cc):
        return jnp.einsum('gcp,gh->gphc', cc.astype(F32), eye).reshape(N_STATE, D_SSM).astype(BF16)

    return (in_map(bbr), in_map(bbi), out_map(c_re), out_map(c_im),
            d_skip.astype(F32).reshape(1, D_SSM), w_glu.astype(BF16))


def _cmul_add(ar, ai, br, bi, cr, ci):
    return ar * br - ai * bi + cr, ar * bi + ai * br + ci


def _s5_body(u_ref, lbr_ref, lbi_ref, pwr_ref, pwi_ref, wbr_ref, wbi_ref, wcr_ref, wci_ref,
             dsk_ref, wg_ref, o_ref, hr_ref, hi_ref, tr_ref, ti_ref, start_states):
    chunk = u_ref.shape[1]
    seg = chunk // SUBLANES
    half_in = D_SSM // 2
    half_st = N_STATE // 2
    u = u_ref[0]
    ub = u.astype(BF16)
    for wref, href in ((wbr_ref, hr_ref), (wbi_ref, hi_ref)):
        href[:, :half_st] = jnp.dot(ub[:, :half_in], wref[:half_in, :half_st],
                                    preferred_element_type=F32)
        href[:, half_st:] = jnp.dot(ub[:, half_in:], wref[half_in:, half_st:],
                                    preferred_element_type=F32)

    ends_r, ends_i = [], []
    for cb in range(N_STATE // S5_COLS):
        cs = slice(cb * S5_COLS, (cb + 1) * S5_COLS)
        lr = jnp.broadcast_to(lbr_ref[:, cs], (SUBLANES, S5_COLS))
        li = jnp.broadcast_to(lbi_ref[:, cs], (SUBLANES, S5_COLS))

        def scan_step(j, carry):
            sr, si = carry
            rows = pl.ds(pl.multiple_of(j * SUBLANES, SUBLANES), SUBLANES)
            sr, si = _cmul_add(lr, li, sr, si, hr_ref[rows, cs], hi_ref[rows, cs])
            hr_ref[rows, cs] = sr
            hi_ref[rows, cs] = si
            return sr, si

        zero = jnp.zeros((SUBLANES, S5_COLS), F32)
        er, ei = lax.fori_loop(0, seg, scan_step, (zero, zero), unroll=min(seg, 4))
        ends_r.append(er)
        ends_i.append(ei)
    end_r = jnp.concatenate(ends_r, axis=1)
    end_i = jnp.concatenate(ends_i, axis=1)
    start_states(end_r, end_i)

    for cb in range(N_STATE // S5_COLS):
        cs = slice(cb * S5_COLS, (cb + 1) * S5_COLS)
        tr = tr_ref[:, cs]
        ti = ti_ref[:, cs]

        def fix_step(j, carry):
            rows = pl.ds(pl.multiple_of(j * SUBLANES, SUBLANES), SUBLANES)
            nr, ni = _cmul_add(pwr_ref[rows, cs], pwi_ref[rows, cs], tr, ti,
                               hr_ref[rows, cs], hi_ref[rows, cs])
            hr_ref[rows, cs] = nr
            hi_ref[rows, cs] = ni
            return carry

        lax.fori_loop(0, seg, fix_step, 0, unroll=min(seg, 4))

    half_out = D_SSM // 2
    ys = []
    for k in range(2):
        st = slice(k * half_st, (k + 1) * half_st)
        oc = slice(k * half_out, (k + 1) * half_out)
        ys.append(jnp.dot(hr_ref[:, st].astype(BF16), wcr_ref[st, oc], preferred_element_type=F32)
                  - jnp.dot(hi_ref[:, st].astype(BF16), wci_ref[st, oc], preferred_element_type=F32))
    y = jnp.concatenate(ys, axis=1) + dsk_ref[...] * u
    z = jax.nn.gelu(y, approximate=True)
    o_ref[0] = z * jax.nn.sigmoid(jnp.dot(z.astype(BF16), wg_ref[...], preferred_element_type=F32))


def _s5_seq_kernel(u_ref, lbr_ref, lbi_ref, pwr_ref, pwi_ref, wbr_ref, wbi_ref, wcr_ref, wci_ref,
                   dsk_ref, wg_ref, o_ref, hlr_ref, hli_ref,
                   hr_ref, hi_ref, tr_ref, ti_ref, cr_ref, ci_ref):
    chunk = u_ref.shape[1]

    @pl.when(pl.program_id(1) == 0)
    def _():
        cr_ref[...] = jnp.zeros_like(cr_ref)
        ci_ref[...] = jnp.zeros_like(ci_ref)

    def start_states(end_r, end_i):
        last = chunk - SUBLANES
        lsr = pwr_ref[last:last + 1, :]
        lsi = pwi_ref[last:last + 1, :]
        sr = cr_ref[...]
        si = ci_ref[...]
        for s in range(SUBLANES):
            tr_ref[s:s + 1, :] = sr
            ti_ref[s:s + 1, :] = si
            sr, si = _cmul_add(lsr, lsi, sr, si, end_r[s:s + 1, :], end_i[s:s + 1, :])
        cr_ref[...] = sr
        ci_ref[...] = si

    _s5_body(u_ref, lbr_ref, lbi_ref, pwr_ref, pwi_ref, wbr_ref, wbi_ref, wcr_ref, wci_ref,
             dsk_ref, wg_ref, o_ref, hr_ref, hi_ref, tr_ref, ti_ref, start_states)
    hlr_ref[0] = cr_ref[...]
    hli_ref[0] = ci_ref[...]


def _s5_batch_kernel(u_ref, h0r_ref, h0i_ref, lbr_ref, lbi_ref, pwr_ref, pwi_ref, wbr_ref, wbi_ref,
                     wcr_ref, wci_ref, dsk_ref, wg_ref, o_ref, hlr_ref, hli_ref,
                     hr_ref, hi_ref, tr_ref, ti_ref):
    chunk = u_ref.shape[1]

    def start_states(end_r, end_i):
        last = chunk - SUBLANES
        tr = h0r_ref[0]
        ti = h0i_ref[0]
        tr_ref[...] = tr
        ti_ref[...] = ti
        fr, fi = _cmul_add(pwr_ref[last:last + SUBLANES, :], pwi_ref[last:last + SUBLANES, :],
                           tr, ti, end_r, end_i)
        hlr_ref[0] = fr
        hli_ref[0] = fi

    _s5_body(u_ref, lbr_ref, lbi_ref, pwr_ref, pwi_ref, wbr_ref, wbi_ref, wcr_ref, wci_ref,
             dsk_ref, wg_ref, o_ref, hr_ref, hi_ref, tr_ref, ti_ref, start_states)


def _to_segment_rows(u, chunk):
    *lead, s, d = u.shape
    seg = chunk // SUBLANES
    x = u.reshape(*lead, s // chunk, SUBLANES, seg, d)
    return jnp.swapaxes(x, -2, -3).reshape(*lead, s, d)


def _from_segment_rows(u, chunk):
    *lead, s, d = u.shape
    seg = chunk // SUBLANES
    x = u.reshape(*lead, s // chunk, seg, SUBLANES, d)
    return jnp.swapaxes(x, -2, -3).reshape(*lead, s, d)


def _s5_scratch(chunk):
    return [pltpu.VMEM((chunk, N_STATE), F32), pltpu.VMEM((chunk, N_STATE), F32),
            pltpu.VMEM((SUBLANES, N_STATE), F32), pltpu.VMEM((SUBLANES, N_STATE), F32)]


def _s5_seq(u, prep, weights, chunk):
    n, s, d = u.shape
    lbr, lbi, _, _, pwr, pwi = prep
    consts = (lbr, lbi, pwr, pwi) + tuple(weights)
    state = jax.ShapeDtypeStruct((n, 1, N_STATE), F32)
    return pl.pallas_call(
        _s5_seq_kernel, name="s5_seq",
        grid=(n, s // chunk),
        in_specs=[pl.BlockSpec((1, chunk, d), lambda b, i: (b, i, 0))]
                 + [_const_spec(a.shape) for a in consts],
        out_specs=(pl.BlockSpec((1, chunk, d), lambda b, i: (b, i, 0)),
                   pl.BlockSpec((1, 1, N_STATE), lambda b, i: (b, 0, 0)),
                   pl.BlockSpec((1, 1, N_STATE), lambda b, i: (b, 0, 0))),
        out_shape=(jax.ShapeDtypeStruct((n, s, d), F32), state, state),
        scratch_shapes=_s5_scratch(chunk) + [pltpu.VMEM((1, N_STATE), F32), pltpu.VMEM((1, N_STATE), F32)],
        compiler_params=_cparams(("arbitrary", "arbitrary")),
    )(u, *consts)


def _s5_batch(u, h0r, h0i, prep, weights):
    nblk, chunk, d = u.shape
    lbr, lbi, _, _, pwr, pwi = prep
    consts = (lbr, lbi, pwr, pwi) + tuple(weights)
    state = jax.ShapeDtypeStruct((nblk, SUBLANES, N_STATE), F32)
    blk = lambda shape: pl.BlockSpec(shape, lambda i: (i, 0, 0))
    return pl.pallas_call(
        _s5_batch_kernel, name="s5_batch",
        grid=(nblk,),
        in_specs=[blk((1, chunk, d)), blk((1, SUBLANES, N_STATE)), blk((1, SUBLANES, N_STATE))]
                 + [_const_spec(a.shape) for a in consts],
        out_specs=(blk((1, chunk, d)), blk((1, SUBLANES, N_STATE)), blk((1, SUBLANES, N_STATE))),
        out_shape=(jax.ShapeDtypeStruct((nblk, chunk, d), F32), state, state),
        scratch_shapes=_s5_scratch(chunk),
        compiler_params=_cparams(("arbitrary",)),
    )(u, h0r, h0i, *consts)


_PEER_CELLS = [(i, j) for i in range(PEER_TOPK) for j in range(PEER_TOPK)
               if (i + 1) * (j + 1) <= PEER_TOPK]
_N_CELLS_PAD = -(-len(_PEER_CELLS) // SUBLANES) * SUBLANES
_BIG_ID = 1 << 20


def _peer_route_kernel(xn_ref, wq_ref, keys_ref, cell_ref, a_ref, b_ref, w_ref,
                       q_ref, sv_ref, si_ref, cand_ref, cv_ref, fid_ref, ao_ref, bo_ref, wo_ref):
    t = xn_ref.shape[0]
    q_ref[...] = jnp.dot(xn_ref[...], wq_ref[...], preferred_element_type=F32)
    key_row = lax.broadcasted_iota(jnp.int32, (N_KEYS, t), 0)

    def sub_topk(ht, carry):
        qh = q_ref[:, pl.ds(pl.multiple_of(ht * D_HALF, D_HALF), D_HALF)].astype(BF16)
        s = lax.dot_general(keys_ref[ht], qh, (((1,), (1,)), ((), ())),
                            preferred_element_type=F32)
        for r in range(PEER_TOPK):
            m = jnp.max(s, axis=0, keepdims=True)
            idx = jnp.min(jnp.where(s == m, key_row, N_KEYS), axis=0, keepdims=True)
            s = jnp.where(key_row == idx, NEG_INF, s)
            sv_ref[ht, r:r + 1, :] = m
            si_ref[ht, r:r + 1, :] = idx
        return carry

    lax.fori_loop(0, 2 * PEER_HEADS, sub_topk, 0)

    cell_row = lax.broadcasted_iota(jnp.int32, (_N_CELLS_PAD, t), 0)
    cell_id = cell_ref[...]

    def combine(h, carry):
        va = sv_ref[2 * h]
        vb = sv_ref[2 * h + 1]
        cand_ref[...] = jnp.full((_N_CELLS_PAD, t), NEG_INF, F32)
        for c, (i, j) in enumerate(_PEER_CELLS):
            cand_ref[c:c + 1, :] = va[i:i + 1, :] + vb[j:j + 1, :]
        cand = cand_ref[...]
        for r in range(PEER_TOPK):
            m = jnp.max(cand, axis=0, keepdims=True)
            ridx = jnp.min(jnp.where(cand == m, cell_row, _N_CELLS_PAD), axis=0, keepdims=True)
            hit = cell_row == ridx
            cand = jnp.where(hit, NEG_INF, cand)
            cv_ref[r:r + 1, :] = m
            fid_ref[r:r + 1, :] = jnp.max(jnp.where(hit, cell_id, 0), axis=0, keepdims=True)
        cv = cv_ref[...]
        fid = fid_ref[...]
        isel = lax.shift_right_logical(fid, 4)
        jsel = jnp.bitwise_and(fid, PEER_TOPK - 1)
        ia = si_ref[2 * h]
        ib = si_ref[2 * h + 1]
        ka = jnp.zeros((PEER_TOPK, t), jnp.int32)
        kb = jnp.zeros((PEER_TOPK, t), jnp.int32)
        for r in range(PEER_TOPK):
            ka = jnp.where(isel == r, ia[r:r + 1, :], ka)
            kb = jnp.where(jsel == r, ib[r:r + 1, :], kb)
        e = jnp.exp(cv - cv[0:1, :])
        ao_ref[h] = ka
        bo_ref[h] = kb
        wo_ref[h] = e / jnp.sum(e, axis=0, keepdims=True)
        return carry

    lax.fori_loop(0, PEER_HEADS, combine, 0)
    nk = PEER_HEADS * PEER_TOPK
    a_ref[...] = ao_ref[...].reshape(nk, t).T
    b_ref[...] = bo_ref[...].reshape(nk, t).T
    w_ref[...] = wo_ref[...].reshape(nk, t).T


def _peer_route(xn, wq, keys, t):
    n, d = xn.shape
    nk = PEER_HEADS * PEER_TOPK
    ids = [i * PEER_TOPK + j for i, j in _PEER_CELLS] + [_BIG_ID] * (_N_CELLS_PAD - len(_PEER_CELLS))
    cell = jnp.broadcast_to(jnp.asarray(ids, jnp.int32)[:, None], (_N_CELLS_PAD, t))
    tok = pl.BlockSpec((t, nk), lambda i: (i, 0))
    return pl.pallas_call(
        _peer_route_kernel, name="peer_route",
        grid=(n // t,),
        in_specs=[pl.BlockSpec((t, d), lambda i: (i, 0)), _const_spec(wq.shape),
                  _const_spec(keys.shape), _const_spec(cell.shape)],
        out_specs=(tok, tok, tok),
        out_shape=(jax.ShapeDtypeStruct((n, nk), jnp.int32), jax.ShapeDtypeStruct((n, nk), jnp.int32),
                   jax.ShapeDtypeStruct((n, nk), F32)),
        scratch_shapes=[pltpu.VMEM((t, wq.shape[1]), F32),
                        pltpu.VMEM((2 * PEER_HEADS, PEER_TOPK, t), F32),
                        pltpu.VMEM((2 * PEER_HEADS, PEER_TOPK, t), jnp.int32),
                        pltpu.VMEM((_N_CELLS_PAD, t), F32),
                        pltpu.VMEM((PEER_TOPK, t), F32), pltpu.VMEM((PEER_TOPK, t), jnp.int32),
                        pltpu.VMEM((PEER_HEADS, PEER_TOPK, t), jnp.int32),
                        pltpu.VMEM((PEER_HEADS, PEER_TOPK, t), jnp.int32),
                        pltpu.VMEM((PEER_HEADS, PEER_TOPK, t), F32)],
        compiler_params=_cparams(("arbitrary",)),
    )(xn, wq, keys, cell)


EXPERT_BLOCK = 2048
COEF_PITCH = 72
HALF_GROUPS = N_KEYS // 2


def _peer_hidden_kernel(xn_ref, a_ref, b_ref, w_ref, u_ref, c_ref, hsel_ref):
    j = pl.program_id(1)
    groups = u_ref.shape[0] // N_KEYS

    @pl.when(j == 0)
    def _():
        hsel_ref[...] = jnp.zeros_like(hsel_ref)

    h = lax.dot_general(xn_ref[...], u_ref[...], (((1,), (1,)), ((), ())),
                        preferred_element_type=F32)
    a_idx = a_ref[...]
    b_idx = b_ref[...]
    hs = hsel_ref[...]
    for g in range(groups):
        hg = h[:, g * N_KEYS:(g + 1) * N_KEYS]
        hs = jnp.where(a_idx == j * groups + g, jnp.take_along_axis(hg, b_idx, axis=1), hs)
    hsel_ref[...] = hs

    @pl.when(j == pl.num_programs(1) - 1)
    def _():
        c_ref[...] = w_ref[...] * jax.nn.gelu(hs, approximate=True)


def _peer_hidden(xn, a_idx, b_idx, w, u_bf16, t):
    n, d = xn.shape
    nk = a_idx.shape[1]
    ne = u_bf16.shape[0]
    tok = lambda width: pl.BlockSpec((t, width), lambda i, j: (i, 0))
    return pl.pallas_call(
        _peer_hidden_kernel, name="peer_hidden",
        grid=(n // t, ne // EXPERT_BLOCK),
        in_specs=[tok(d), tok(nk), tok(nk), tok(nk),
                  pl.BlockSpec((EXPERT_BLOCK, d), lambda i, j: (j, 0))],
        out_specs=tok(nk),
        out_shape=jax.ShapeDtypeStruct((n, nk), F32),
        scratch_shapes=[pltpu.VMEM((t, nk), F32)],
        compiler_params=_cparams(("arbitrary", "arbitrary")),
    )(xn, a_idx, b_idx, w, u_bf16)


def _peer_out_kernel(a_ref, b_ref, c_ref, x1_ref, v_ref, y_ref, s_ref, acc_ref):
    j = pl.program_id(1)
    t = a_ref.shape[0]
    pairs = v_ref.shape[0] // (2 * N_KEYS)
    hi_mask = jnp.uint32(0xFFFF0000)

    @pl.when(j == 0)
    def _():
        sub = lax.broadcasted_iota(jnp.int32, (N_KEYS, N_KEYS), 0)

        def one_token(i, carry):
            arow = a_ref[pl.ds(i, 1), :]
            brow = b_ref[pl.ds(i, 1), :]
            crow = c_ref[pl.ds(i, 1), :]
            pt = jnp.where(sub == arow, 1.0, 0.0).astype(BF16)
            rt = jnp.where(sub == brow, crow, 0.0).astype(BF16)
            coef = lax.dot_general(pt, rt, (((1,), (1,)), ((), ())), preferred_element_type=F32)
            bits = pltpu.bitcast(coef.astype(BF16).astype(F32), jnp.uint32)
            packed = jnp.bitwise_or(jnp.bitwise_and(bits[:HALF_GROUPS], hi_mask),
                                    lax.shift_right_logical(bits[HALF_GROUPS:], jnp.uint32(16)))
            s_ref[pl.ds(pl.multiple_of(i * COEF_PITCH, SUBLANES), HALF_GROUPS), :] = packed
            return carry

        lax.fori_loop(0, t, one_token, 0)

    parts = []
    for g in range(pairs):
        w32 = s_ref[pl.ds(j * pairs + g, t, stride=COEF_PITCH), :]
        parts.append(pltpu.bitcast(jnp.bitwise_and(w32, hi_mask), F32).astype(BF16))
        parts.append(pltpu.bitcast(lax.shift_left(w32, jnp.uint32(16)), F32).astype(BF16))
    coef_blk = jnp.concatenate(parts, axis=1)
    contrib = jnp.dot(coef_blk, v_ref[...], preferred_element_type=F32)

    @pl.when(j == 0)
    def _():
        acc_ref[...] = contrib

    @pl.when(j > 0)
    def _():
        acc_ref[...] += contrib

    @pl.when(j == pl.num_programs(1) - 1)
    def _():
        y_ref[...] = x1_ref[...] + acc_ref[...]


def _pair_groups(table_bf16):
    ne, d = table_bf16.shape
    x = table_bf16.reshape(2, HALF_GROUPS, N_KEYS, d)
    return jnp.swapaxes(x, 0, 1).reshape(ne, d)


def _peer_out(a_idx, b_idx, c, x1, v_paired, t):
    n, d = x1.shape
    nk = a_idx.shape[1]
    ne = v_paired.shape[0]
    tok = lambda width: pl.BlockSpec((t, width), lambda i, j: (i, 0))
    return pl.pallas_call(
        _peer_out_kernel, name="peer_out",
        grid=(n // t, ne // EXPERT_BLOCK),
        in_specs=[tok(nk), tok(nk), tok(nk), tok(d),
                  pl.BlockSpec((EXPERT_BLOCK, d), lambda i, j: (j, 0))],
        out_specs=tok(d),
        out_shape=jax.ShapeDtypeStruct((n, d), F32),
        scratch_shapes=[pltpu.VMEM((t * COEF_PITCH, N_KEYS), jnp.uint32), pltpu.VMEM((t, d), F32)],
        compiler_params=_cparams(("arbitrary", "arbitrary")),
    )(a_idx, b_idx, c, x1, v_paired)


TOKEN_TILE = 512
ATTN_TILE = 512
S5_CHUNK = 256
ROUTE_TILE = 256
EXPERT_TILE = 512


def _tile(n, pref):
    t = min(pref, n)
    while n % t:
        t //= 2
    return t


def _tail(x1, xn2, lw):
    n = x1.shape[0]
    a_idx, b_idx, w = _peer_route(xn2, lw['wq'], lw['keys'], _tile(n, ROUTE_TILE))
    te = _tile(n, EXPERT_TILE)
    c = _peer_hidden(xn2, a_idx, b_idx, w, lw['u'], te)
    return _peer_out(a_idx, b_idx, c, x1, lw['v'], te)


def _layer_weights(l, norm1_g, w_in, b_forget, q_norm_g, k_norm_g, ssm_lam_re, ssm_lam_im, ssm_log_dt,
                   ssm_b_re, ssm_b_im, ssm_c_re, ssm_c_im, ssm_d, w_glu, w_proj_attn, w_proj_ssm, w_out,
                   norm2_g, w_query, sub_keys, expert_u, expert_v, chunks):
    lw = {'inproj': _prep_inproj(norm1_g[l], w_in[l], b_forget[l], q_norm_g[l], k_norm_g[l])}
    lw['s5_prep'] = {c: _s5_prep(ssm_lam_re[l], ssm_lam_im[l], ssm_log_dt[l], ssm_b_re[l], ssm_b_im[l], c)
                     for c in chunks}
    p0 = lw['s5_prep'][chunks[0]]
    lw['s5_w'] = _s5_weights(p0[2], p0[3], ssm_c_re[l], ssm_c_im[l], ssm_d[l], w_glu[l])
    lw['mix'] = (w_proj_attn[l].astype(BF16), w_proj_ssm[l].astype(BF16), w_out[l].astype(BF16),
                 norm2_g[l].astype(F32)[None, :])
    lw['wq'] = w_query[l].astype(BF16)
    lw['keys'] = sub_keys[l].reshape(2 * PEER_HEADS, N_KEYS, D_HALF).astype(BF16)
    lw['u'] = expert_u[l].astype(BF16)
    lw['v'] = _pair_groups(expert_v[l].astype(BF16))
    return lw


def kernel(x_prompt, x_sample, cache_k, cache_v, cache_logf, state_ssm_re, state_ssm_im, page_table,
           norm1_g, w_in, b_forget, q_norm_g, k_norm_g,
           ssm_lam_re, ssm_lam_im, ssm_log_dt, ssm_b_re, ssm_b_im, ssm_c_re, ssm_c_im, ssm_d, w_glu,
           w_proj_attn, w_proj_ssm, w_out, norm2_g, w_query, sub_keys, expert_u, expert_v):
    nb, s, d = x_prompt.shape
    db, t, _ = x_sample.shape
    depth = w_in.shape[0]
    assert db % SUBLANES == 0 and t * N_HEADS <= LANES and t <= PAGE
    batch_chunk = SUBLANES * t
    y_p = x_prompt.reshape(nb * s, d)
    y_s = x_sample.reshape(db * t, d)
    outs = [[] for _ in range(10)]
    for l in range(depth):
        lw = _layer_weights(l, norm1_g, w_in, b_forget, q_norm_g, k_norm_g, ssm_lam_re, ssm_lam_im,
                            ssm_log_dt, ssm_b_re, ssm_b_im, ssm_c_re, ssm_c_im, ssm_d, w_glu,
                            w_proj_attn, w_proj_ssm, w_out, norm2_g, w_query, sub_keys, expert_u,
                            expert_v, (S5_CHUNK, batch_chunk))

        tm = _tile(nb * s, TOKEN_TILE)
        qb, kb, vb, kf, vf, lf, u, ga, gs = _inproj(y_p, *lw['inproj'], tm=tm)
        lf3 = lf.reshape(nb, s, N_HEADS)
        c, ct = _cumsum(lf3, jnp.swapaxes(lf3, 1, 2), ATTN_TILE)
        o_attn = _attn_prompt(qb.reshape(nb, s, D_ATTN), kb.reshape(nb, s, D_ATTN),
                              vb.reshape(nb, s, D_ATTN), c, ct, ATTN_TILE)
        u_seg = _to_segment_rows(u.reshape(nb, s, D_SSM), S5_CHUNK)
        o_ssm, hr, hi = _s5_seq(u_seg, lw['s5_prep'][S5_CHUNK], lw['s5_w'], S5_CHUNK)
        o_ssm = _from_segment_rows(o_ssm, S5_CHUNK).reshape(nb * s, D_SSM)
        x1, xn2 = _mix(o_attn.reshape(nb * s, D_ATTN), o_ssm, ga, gs, y_p, *lw['mix'], tm=tm)
        y_p = _tail(x1, xn2, lw)
        for slot, val in zip(range(5), (kf.reshape(nb, s, N_HEADS, HEAD_DIM),
                                        vf.reshape(nb, s, N_HEADS, HEAD_DIM), lf3,
                                        hr.reshape(nb, N_GROUPS, STATE_DIM),
                                        hi.reshape(nb, N_GROUPS, STATE_DIM))):
            outs[slot].append(val)

        tm = _tile(db * t, TOKEN_TILE)
        qb, kb, vb, kf, vf, lf, u, ga, gs = _inproj(y_s, *lw['inproj'], tm=tm)
        qm = jnp.swapaxes(qb.reshape(db, t * N_HEADS, HEAD_DIM), 1, 2)
        qm = jnp.pad(qm, ((0, 0), (0, 0), (0, LANES - t * N_HEADS)))
        pad_page = lambda x: jnp.pad(x, ((0, 0), (0, PAGE - t)) + ((0, 0),) * (x.ndim - 2))
        o_cols = _attn_paged(page_table, qm, pad_page(kf.reshape(db, t, N_HEADS, HEAD_DIM)),
                             pad_page(vf.reshape(db, t, N_HEADS, HEAD_DIM)),
                             pad_page(lf.reshape(db, t, N_HEADS)),
                             cache_k, cache_v, cache_logf, t, l)
        o_attn = jnp.swapaxes(o_cols, 1, 2)[:, :t * N_HEADS, :].reshape(db * t, D_ATTN).astype(BF16)
        nblk = db // SUBLANES
        u_seg = _to_segment_rows(u.reshape(nblk, batch_chunk, D_SSM), batch_chunk)
        h0r = state_ssm_re[l].astype(F32).reshape(nblk, SUBLANES, N_STATE)
        h0i = state_ssm_im[l].astype(F32).reshape(nblk, SUBLANES, N_STATE)
        o_ssm, hr, hi = _s5_batch(u_seg, h0r, h0i, lw['s5_prep'][batch_chunk], lw['s5_w'])
        o_ssm = _from_segment_rows(o_ssm, batch_chunk).reshape(db * t, D_SSM)
        x1, xn2 = _mix(o_attn, o_ssm, ga, gs, y_s, *lw['mix'], tm=tm)
        y_s = _tail(x1, xn2, lw)
        for slot, val in zip(range(5, 10), (kf.reshape(db, t, N_HEADS, HEAD_DIM),
                                            vf.reshape(db, t, N_HEADS, HEAD_DIM),
                                            lf.reshape(db, t, N_HEADS),
                                            hr.reshape(db, N_GROUPS, STATE_DIM),
                                            hi.reshape(db, N_GROUPS, STATE_DIM))):
            outs[slot].append(val)
    stacked = [jnp.stack(o) for o in outs]
    return (y_p.reshape(nb, s, d), y_s.reshape(db, t, d), *stacked)
```

```python
import functools
import math

import jax
import jax.numpy as jnp
from jax import lax
from jax.experimental import pallas as pl
from jax.experimental.pallas import tpu as pltpu

F32 = jnp.float32
BF16 = jnp.bfloat16

N_HEADS = 8
HEAD_DIM = 64
D_ATTN = N_HEADS * HEAD_DIM
ATTN_SCALE = HEAD_DIM ** -0.5
LOG2E = math.log2(math.e)
D_SSM = 512
SSM_GROUP = 16
N_GROUPS = D_SSM // SSM_GROUP
STATE_DIM = 64
N_STATE = N_GROUPS * STATE_DIM
PEER_HEADS = 8
N_KEYS = 128
PEER_TOPK = 16
D_HALF = 128
NORM_EPS = 1e-6
NEG_INF = -1e30
PAGE = 128

LANES = 128
SUBLANES = 8
VMEM_LIMIT = 56 * 1024 * 1024


def _cparams(sem):
    return pltpu.CompilerParams(dimension_semantics=sem, vmem_limit_bytes=VMEM_LIMIT)


def _const_spec(shape):
    nd = len(shape)
    return pl.BlockSpec(shape, lambda *_: (0,) * nd)


def _rms_rows(x, g):
    r = lax.rsqrt(jnp.mean(x * x, axis=-1, keepdims=True) + NORM_EPS)
    return (x * r) * g


def _split_dot(a, b_bf16):
    hi = a.astype(BF16)
    lo = (a - hi.astype(F32)).astype(BF16)
    return (jnp.dot(hi, b_bf16, preferred_element_type=F32)
            + jnp.dot(lo, b_bf16, preferred_element_type=F32))


def _log_sigmoid(x):
    return jnp.minimum(x, 0.0) - jnp.log1p(jnp.exp(-jnp.abs(x)))


def _inproj_kernel(x_ref, g1_ref, w_ref, wf_ref, bf_ref, gq_ref, gk_ref, hsel_ref,
                   qb_ref, kb_ref, vb_ref, kf_ref, vf_ref, lf_ref, u_ref, ga_ref, gs_ref):
    xn = _rms_rows(x_ref[...], g1_ref[...]).astype(BF16)

    def seg(i, n):
        return jnp.dot(xn, w_ref[:, i:i + n], preferred_element_type=F32)

    hsel = hsel_ref[...]

    def head_norm(z, g):
        ms = _split_dot(z * z, hsel)
        return (z * lax.rsqrt(ms + NORM_EPS)) * g

    q = head_norm(seg(0, D_ATTN), gq_ref[...])
    qb_ref[...] = (q * (ATTN_SCALE * LOG2E)).astype(BF16)
    k = head_norm(seg(D_ATTN, D_ATTN), gk_ref[...])
    kf_ref[...] = k
    kb_ref[...] = k.astype(BF16)
    v = seg(2 * D_ATTN, D_ATTN)
    vf_ref[...] = v
    vb_ref[...] = v.astype(BF16)
    u_ref[...] = seg(3 * D_ATTN, D_SSM)
    o = 3 * D_ATTN + D_SSM
    d = ga_ref.shape[1]
    ga_ref[...] = jax.nn.sigmoid(seg(o, d)).astype(BF16)
    gs_ref[...] = jax.nn.sigmoid(seg(o + d, d)).astype(BF16)
    zf = jnp.dot(xn, wf_ref[...], preferred_element_type=F32) + bf_ref[...]
    lf_ref[...] = _log_sigmoid(zf)[:, :N_HEADS]


def _prep_inproj(norm1_g, w_in, b_forget, q_norm_g, k_norm_g):
    o3 = 3 * D_ATTN
    o4 = o3 + N_HEADS
    wmain = jnp.concatenate([w_in[:, :o3], w_in[:, o4:]], axis=1).astype(BF16)
    wf = jnp.pad(w_in[:, o3:o4], ((0, 0), (0, LANES - N_HEADS))).astype(BF16)
    bfp = jnp.pad(b_forget.astype(F32), (0, LANES - N_HEADS))[None, :]
    gq = jnp.tile(q_norm_g.astype(F32), N_HEADS)[None, :]
    gk = jnp.tile(k_norm_g.astype(F32), N_HEADS)[None, :]
    hsel = jnp.kron(jnp.eye(N_HEADS, dtype=F32),
                    jnp.full((HEAD_DIM, HEAD_DIM), 1.0 / HEAD_DIM, F32)).astype(BF16)
    return norm1_g.astype(F32)[None, :], wmain, wf, bfp, gq, gk, hsel


def _inproj(x, g1, wmain, wf, bfp, gq, gk, hsel, tm):
    n, d = x.shape
    tok = lambda w: pl.BlockSpec((tm, w), lambda i: (i, 0))
    out_shape = (
        jax.ShapeDtypeStruct((n, D_ATTN), BF16), jax.ShapeDtypeStruct((n, D_ATTN), BF16),
        jax.ShapeDtypeStruct((n, D_ATTN), BF16), jax.ShapeDtypeStruct((n, D_ATTN), F32),
        jax.ShapeDtypeStruct((n, D_ATTN), F32), jax.ShapeDtypeStruct((n, N_HEADS), F32),
        jax.ShapeDtypeStruct((n, D_SSM), F32), jax.ShapeDtypeStruct((n, d), BF16),
        jax.ShapeDtypeStruct((n, d), BF16))
    return pl.pallas_call(
        _inproj_kernel, name="inproj",
        grid=(n // tm,),
        in_specs=[tok(d), _const_spec(g1.shape), _const_spec(wmain.shape), _const_spec(wf.shape),
                  _const_spec(bfp.shape), _const_spec(gq.shape), _const_spec(gk.shape),
                  _const_spec(hsel.shape)],
        out_specs=(tok(D_ATTN), tok(D_ATTN), tok(D_ATTN), tok(D_ATTN), tok(D_ATTN), tok(N_HEADS),
                   tok(D_SSM), tok(d), tok(d)),
        out_shape=out_shape,
        compiler_params=_cparams(("arbitrary",)),
    )(x, g1, wmain, wf, bfp, gq, gk, hsel)


def _mix_kernel(oa_ref, os_ref, ga_ref, gs_ref, x_ref, wpa_ref, wps_ref, wo_ref, g2_ref,
                x1_ref, xn_ref):
    pa = jnp.dot(oa_ref[...], wpa_ref[...], preferred_element_type=F32)
    ps = jnp.dot(os_ref[...].astype(BF16), wps_ref[...], preferred_element_type=F32)
    mixed = ga_ref[...].astype(F32) * pa + gs_ref[...].astype(F32) * ps
    x1 = x_ref[...] + jnp.dot(mixed.astype(BF16), wo_ref[...], preferred_element_type=F32)
    x1_ref[...] = x1
    xn_ref[...] = _rms_rows(x1, g2_ref[...]).astype(BF16)


def _mix(oa, os_, ga, gs, x, wpa, wps, wo, g2, tm):
    n, d = x.shape
    tok = lambda w: pl.BlockSpec((tm, w), lambda i: (i, 0))
    return pl.pallas_call(
        _mix_kernel, name="mix",
        grid=(n // tm,),
        in_specs=[tok(D_ATTN), tok(D_SSM), tok(d), tok(d), tok(d), _const_spec(wpa.shape),
                  _const_spec(wps.shape), _const_spec(wo.shape), _const_spec(g2.shape)],
        out_specs=(tok(d), tok(d)),
        out_shape=(jax.ShapeDtypeStruct((n, d), F32), jax.ShapeDtypeStruct((n, d), BF16)),
        compiler_params=_cparams(("arbitrary",)),
    )(oa, os_, ga, gs, x, wpa, wps, wo, g2)


def _split3(a):
    parts = []
    rem = a
    for _ in range(3):
        part = rem.astype(BF16)
        rem = rem - part.astype(F32)
        parts.append(part)
    return parts


def _dot01_right(a, b01):
    return sum(jnp.dot(p, b01, preferred_element_type=F32) for p in _split3(a))


def _dot01_left(a01, b):
    return sum(jnp.dot(a01, p, preferred_element_type=F32) for p in _split3(b))


def _cumsum_kernel(lft_ref, ct_ref, carry_ref):
    t = lft_ref.shape[2]

    @pl.when(pl.program_id(1) == 0)
    def _():
        carry_ref[...] = jnp.zeros_like(carry_ref)

    r = lax.broadcasted_iota(jnp.int32, (t, t), 0)
    c = lax.broadcasted_iota(jnp.int32, (t, t), 1)
    upto = jnp.where(r <= c, 1.0, 0.0).astype(BF16)
    cs = carry_ref[...] + _dot01_right(lft_ref[0], upto)
    ct_ref[0] = cs * LOG2E
    carry_ref[...] = cs[:, t - 1:t]


def _cumsum(lft, t):
    n, h, s = lft.shape
    return pl.pallas_call(
        _cumsum_kernel, name="logf_cumsum",
        grid=(n, s // t),
        in_specs=[pl.BlockSpec((1, h, t), lambda b, i: (b, 0, i))],
        out_specs=pl.BlockSpec((1, h, t), lambda b, i: (b, 0, i)),
        out_shape=jax.ShapeDtypeStruct((n, h, s), F32),
        scratch_shapes=[pltpu.VMEM((h, 1), F32)],
        compiler_params=_cparams(("arbitrary", "arbitrary")),
    )(lft)


ATTN_ROW_BLOCK = 256


def _attn_prompt_kernel(q_ref, k_ref, v_ref, ct_ref, o_ref, m_ref, acc_ref,
                        s0_ref, s1_ref, p0_ref, p1_ref, a0_ref, a1_ref):
    tq = q_ref.shape[1]
    tk = k_ref.shape[1]
    rb_rows = min(ATTN_ROW_BLOCK, tq)
    n_rb = tq // rb_rows
    n_iter = (N_HEADS // 2) * n_rb
    qi = pl.program_id(1)
    ki = pl.program_id(2)
    s_refs = (s0_ref, s1_ref)
    p_refs = (p0_ref, p1_ref)
    a_refs = (a0_ref, a1_ref)

    @pl.when(ki == 0)
    def _():
        m_ref[...] = jnp.full_like(m_ref, NEG_INF)
        acc_ref[...] = jnp.zeros_like(acc_ref)

    lane = lax.broadcasted_iota(jnp.int32, (tk, LANES), 1)
    keep = (jnp.where(lane < HEAD_DIM, 1.0, 0.0).astype(BF16),
            jnp.where(lane < HEAD_DIM, 0.0, 1.0).astype(BF16))
    ones = (jnp.where(lane == HEAD_DIM, 1.0, 0.0).astype(BF16),
            jnp.where(lane == 0, 1.0, 0.0).astype(BF16))

    def where_is(j):
        pair = j // n_rb
        return pair, pl.multiple_of((j - pair * n_rb) * rb_rows, rb_rows)

    def slab(pair):
        return pl.ds(pl.multiple_of(pair * LANES, LANES), LANES)

    def scores(j, half):
        pair, row0 = where_is(j)
        qblk = q_ref[0, pl.ds(row0, rb_rows), slab(pair)]
        kh = k_ref[0, :, slab(pair)] * keep[half]
        s_refs[half][...] = lax.dot_general(qblk, kh, (((1,), (1,)), ((), ())),
                                            preferred_element_type=F32)

    def softmax(j, half, diagonal):
        pair, row0 = where_is(j)
        h = 2 * pair + half
        ckb = jnp.broadcast_to(ct_ref[0, pl.ds(h, 1), :], (SUBLANES, tk))
        if diagonal:
            col = lax.broadcasted_iota(jnp.int32, (SUBLANES, tk), 1)
            sub = lax.broadcasted_iota(jnp.int32, (SUBLANES, tk), 0) + row0
        probs = None
        for g in range(rb_rows // SUBLANES):
            lo = g * SUBLANES
            s = s_refs[half][lo:lo + SUBLANES, :] - ckb
            if diagonal:
                s = jnp.where(col <= sub + lo, s, NEG_INF)
            rows = pl.ds(row0 + lo, SUBLANES)
            m_prev = m_ref[h, rows, :]
            m_new = jnp.maximum(m_prev, jnp.max(s, axis=-1, keepdims=True))
            a_refs[half][lo:lo + SUBLANES, :] = jnp.exp2(m_prev - m_new)
            m_ref[h, rows, :] = m_new
            prob = jnp.exp2(s - m_new)
            if g % 2 == 0:
                probs = prob
            else:
                p_refs[half][lo - SUBLANES:lo + SUBLANES, :] = (
                    jnp.concatenate([probs, prob], axis=0).astype(BF16))

    def values(j, half):
        pair, row0 = where_is(j)
        h = 2 * pair + half
        vh = v_ref[0, :, slab(pair)] * keep[half] + ones[half]
        pv = jnp.dot(p_refs[half][...], vh, preferred_element_type=F32)
        rows = pl.ds(row0, rb_rows)
        acc_ref[h, rows, :] = a_refs[half][...] * acc_ref[h, rows, :] + pv

    def step(diagonal):
        scores(0, 0)
        scores(0, 1)
        softmax(0, 0, diagonal)

        def body(j, carry):
            scores(j, 0)
            softmax(j - 1, 1, diagonal)
            values(j - 1, 0)
            scores(j, 1)
            softmax(j, 0, diagonal)
            values(j - 1, 1)
            return carry

        lax.fori_loop(1, n_iter, body, 0)
        softmax(n_iter - 1, 1, diagonal)
        values(n_iter - 1, 0)
        values(n_iter - 1, 1)

    @pl.when(ki < qi)
    def _():
        step(False)

    @pl.when(ki == qi)
    def _():
        step(True)
        lane_q = lax.broadcasted_iota(jnp.int32, (tq, LANES), 1) < HEAD_DIM
        for hp in range(N_HEADS // 2):
            lo = acc_ref[2 * hp]
            hi = acc_ref[2 * hp + 1]
            out = jnp.where(lane_q, lo / lo[:, HEAD_DIM:HEAD_DIM + 1], hi / hi[:, 0:1])
            o_ref[0, :, hp * LANES:(hp + 1) * LANES] = out.astype(o_ref.dtype)


def _attn_prompt(qb, kb, vb, ct, t):
    n, s, d = qb.shape
    nb = s // t
    rb = min(ATTN_ROW_BLOCK, t)
    qmap = lambda b, i, j: (b, i, 0)
    kmap = lambda b, i, j: (b, jnp.minimum(i, j), 0)
    return pl.pallas_call(
        _attn_prompt_kernel, name="attn_prompt",
        grid=(n, nb, nb),
        in_specs=[pl.BlockSpec((1, t, d), qmap), pl.BlockSpec((1, t, d), kmap),
                  pl.BlockSpec((1, t, d), kmap),
                  pl.BlockSpec((1, N_HEADS, t), lambda b, i, j: (b, 0, jnp.minimum(i, j)))],
        out_specs=pl.BlockSpec((1, t, d), qmap),
        out_shape=jax.ShapeDtypeStruct((n, s, d), BF16),
        scratch_shapes=[pltpu.VMEM((N_HEADS, t, 1), F32), pltpu.VMEM((N_HEADS, t, LANES), F32),
                        pltpu.VMEM((rb, t), F32), pltpu.VMEM((rb, t), F32),
                        pltpu.VMEM((rb, t), BF16), pltpu.VMEM((rb, t), BF16),
                        pltpu.VMEM((rb, 1), F32), pltpu.VMEM((rb, 1), F32)],
        compiler_params=_cparams(("arbitrary", "arbitrary", "arbitrary")),
    )(qb, kb, vb, ct)


PAGES_PER_STEP = 16


def _attn_paged_kernel(*refs, n_new, group):
    pt_ref, q_ref, kn_ref, vn_ref, lfn_ref = refs[:5]
    kc_refs = refs[5:5 + group]
    vc_refs = refs[5 + group:5 + 2 * group]
    lfc_refs = refs[5 + 2 * group:5 + 3 * group]
    o_ref, m_ref, l_ref, acc_ref, carry_ref, cq_ref = refs[5 + 3 * group:]
    del pt_ref
    p = pl.program_id(1)
    rows = n_new * N_HEADS
    krow = lax.broadcasted_iota(jnp.int32, (PAGE, PAGE), 0)
    kcol = lax.broadcasted_iota(jnp.int32, (PAGE, PAGE), 1)
    later = jnp.where(krow > kcol, 1.0, 0.0).astype(BF16)
    qbd = q_ref[0]

    def suffix_in_page(lf_page):
        loc = _dot01_right(lf_page, later)
        return loc, loc[:, 0:1] + lf_page[:, 0:1]

    def visit(pages, carry, valid):
        scores, vals = [], []
        for k_page, v_page, lf_page in pages:
            loc, total = suffix_in_page(lf_page)
            bias = jnp.concatenate([(loc + carry) * LOG2E] * n_new, axis=0)
            s = jnp.dot(qbd, k_page.astype(BF16), preferred_element_type=F32) + bias
            scores.append(s)
            vals.append(v_page.astype(BF16))
            carry = carry + total
        s = jnp.concatenate(scores, axis=1)
        if valid is not None:
            s = jnp.where(valid, s, NEG_INF)
        cq = cq_ref[...]
        m_prev = m_ref[...]
        m_new = jnp.maximum(m_prev, jnp.max(s, axis=1, keepdims=True) + cq)
        prob = jnp.exp2(s - (m_new - cq))
        alpha = jnp.exp2(m_prev - m_new)
        l_ref[...] = alpha * l_ref[...] + jnp.sum(prob, axis=1, keepdims=True)
        m_ref[...] = m_new
        pb = prob.astype(BF16)
        pv = None
        for i, v in enumerate(vals):
            t = lax.dot_general(pb[:, i * PAGE:(i + 1) * PAGE], v, (((1,), (1,)), ((), ())),
                                preferred_element_type=F32)
            pv = t if pv is None else pv + t
        acc_ref[...] = alpha * acc_ref[...] + pv
        return carry

    @pl.when(p == 0)
    def _():
        row = lax.broadcasted_iota(jnp.int32, (rows, PAGE), 0)
        col = lax.broadcasted_iota(jnp.int32, (rows, PAGE), 1)
        qry = lax.shift_right_logical(row, 3)
        lf_new = lfn_ref[0]
        loc, _ = suffix_in_page(lf_new)
        tiled = jnp.concatenate([loc] * n_new, axis=0)
        cq_ref[...] = -LOG2E * jnp.sum(jnp.where(col == qry, tiled, 0.0), axis=1, keepdims=True)
        m_ref[...] = jnp.full_like(m_ref, NEG_INF)
        l_ref[...] = jnp.zeros_like(l_ref)
        acc_ref[...] = jnp.zeros_like(acc_ref)
        carry_ref[...] = visit([(kn_ref[0], vn_ref[0], lf_new)], jnp.zeros((N_HEADS, 1), F32),
                               col <= qry)

    @pl.when(p > 0)
    def _():
        pages = [(kc_refs[i][0, 0], vc_refs[i][0, 0], lfc_refs[i][0, 0]) for i in range(group)]
        carry_ref[...] = visit(pages, carry_ref[...], None)

    @pl.when(p == pl.num_programs(1) - 1)
    def _():
        o_ref[0] = acc_ref[...] / l_ref[...]


def _attn_paged(page_table, qb, kf, vf, lf, cache_k, cache_v, cache_logf, n_new, layer):
    nb, n_pages = page_table.shape
    rows = n_new * N_HEADS
    group = _tile(n_pages, PAGES_PER_STEP)
    ck = jnp.transpose(cache_k, (0, 1, 3, 4, 2)).reshape(cache_k.shape[0], -1, D_ATTN, PAGE)
    cv = jnp.transpose(cache_v, (0, 1, 3, 4, 2)).reshape(cache_v.shape[0], -1, D_ATTN, PAGE)
    cl = jnp.transpose(cache_logf, (0, 1, 3, 2))
    q4 = qb.reshape(nb, n_new, N_HEADS, HEAD_DIM)
    qbd = jnp.einsum('bthd,gh->btghd', q4, jnp.eye(N_HEADS, dtype=qb.dtype)).reshape(nb, rows, D_ATTN)
    key_minor = lambda x: jnp.pad(jnp.swapaxes(x.reshape(nb, n_new, -1), 1, 2),
                                  ((0, 0), (0, 0), (0, PAGE - n_new)))
    per_seq = lambda b, p, pt: (b, 0, 0)

    def page(i):
        return lambda b, p, pt: (layer, pt[b, n_pages - 1 - (jnp.maximum(p, 1) - 1) * group - i], 0, 0)

    in_specs = [pl.BlockSpec((1, rows, D_ATTN), per_seq), pl.BlockSpec((1, D_ATTN, PAGE), per_seq),
                pl.BlockSpec((1, D_ATTN, PAGE), per_seq), pl.BlockSpec((1, N_HEADS, PAGE), per_seq)]
    in_specs += [pl.BlockSpec((1, 1, D_ATTN, PAGE), page(i)) for i in range(group)] * 2
    in_specs += [pl.BlockSpec((1, 1, N_HEADS, PAGE), page(i)) for i in range(group)]
    grid_spec = pltpu.PrefetchScalarGridSpec(
        num_scalar_prefetch=1,
        grid=(nb, n_pages // group + 1),
        in_specs=in_specs,
        out_specs=pl.BlockSpec((1, rows, D_ATTN), per_seq),
        scratch_shapes=[pltpu.VMEM((rows, 1), F32), pltpu.VMEM((rows, 1), F32),
                        pltpu.VMEM((rows, D_ATTN), F32), pltpu.VMEM((N_HEADS, 1), F32),
                        pltpu.VMEM((rows, 1), F32)])
    o = pl.pallas_call(
        functools.partial(_attn_paged_kernel, n_new=n_new, group=group), name="attn_paged",
        grid_spec=grid_spec,
        out_shape=jax.ShapeDtypeStruct((nb, rows, D_ATTN), F32),
        compiler_params=_cparams(("arbitrary", "arbitrary")),
    )(page_table, qbd, key_minor(kf), key_minor(vf), key_minor(lf),
      *([ck] * group), *([cv] * group), *([cl] * group))
    o5 = o.reshape(nb, n_new, N_HEADS, N_HEADS, HEAD_DIM)
    return jnp.einsum('btggd->btgd', o5).reshape(nb * n_new, D_ATTN)


S5_COLS = 512


def _s5_prep_kernel(lre_ref, lim_ref, ldt_ref, bre_ref, bim_ref, jp_ref,
                    lbr_ref, lbi_ref, bbr_ref, bbi_ref, pwr_ref, pwi_ref):
    lre = lre_ref[...]
    lim = lim_ref[...]
    dt = jnp.exp(ldt_ref[...])
    mag = jnp.exp(lre * dt)
    ang = lim * dt
    lbr = mag * jnp.cos(ang)
    lbi = mag * jnp.sin(ang)
    lbr_ref[...] = lbr
    lbi_ref[...] = lbi
    nr = lbr - 1.0
    den = lre * lre + lim * lim
    cr = (nr * lre + lbi * lim) / den
    ci = (lbi * lre - nr * lim) / den
    bre = bre_ref[...]
    bim = bim_ref[...]
    bbr_ref[...] = cr * bre - ci * bim
    bbi_ref[...] = cr * bim + ci * bre
    jp = jp_ref[...]
    pmag = jnp.exp(jp * (lre * dt))
    pang = jp * ang
    pwr_ref[...] = pmag * jnp.cos(pang)
    pwi_ref[...] = pmag * jnp.sin(pang)


def _s5_prep(lam_re, lam_im, log_dt, b_re, b_im, chunk):
    lre = lam_re.reshape(1, N_STATE).astype(F32)
    lim = lam_im.reshape(1, N_STATE).astype(F32)
    ldt = jnp.repeat(log_dt.astype(F32), STATE_DIM).reshape(1, N_STATE)
    bre = jnp.transpose(b_re.astype(F32), (2, 0, 1)).reshape(SSM_GROUP, N_STATE)
    bim = jnp.transpose(b_im.astype(F32), (2, 0, 1)).reshape(SSM_GROUP, N_STATE)
    jp = (jnp.arange(chunk, dtype=jnp.int32) // SUBLANES + 1).astype(F32).reshape(chunk, 1)
    vec = jax.ShapeDtypeStruct((1, N_STATE), F32)
    mat = jax.ShapeDtypeStruct((SSM_GROUP, N_STATE), F32)
    tab = jax.ShapeDtypeStruct((chunk, N_STATE), F32)
    return pl.pallas_call(
        _s5_prep_kernel, name="s5_prep",
        out_shape=(vec, vec, mat, mat, tab, tab),
        compiler_params=pltpu.CompilerParams(vmem_limit_bytes=VMEM_LIMIT),
    )(lre, lim, ldt, bre, bim, jp)


def _s5_weights(bbr, bbi, c_re, c_im, d_skip, w_glu):
    eye = jnp.eye(N_GROUPS, dtype=F32)

    def in_map(bb):
        b3 = bb.reshape(SSM_GROUP, N_GROUPS, STATE_DIM)
        return jnp.einsum('cgp,gh->gchp', b3, eye).reshape(D_SSM, N_STATE).astype(BF16)

    def out_map(cc):
        return jnp.einsum('gcp,gh->gphc', cc.astype(F32), eye).reshape(N_STATE, D_SSM).astype(BF16)

    return (in_map(bbr), in_map(bbi), out_map(c_re), out_map(c_im),
            d_skip.astype(F32).reshape(1, D_SSM), w_glu.astype(BF16))


def _cmul_add(ar, ai, br, bi, cr, ci):
    return ar * br - ai * bi + cr, ar * bi + ai * br + ci


def _s5_body(u_ref, lbr_ref, lbi_ref, pwr_ref, pwi_ref, wbr_ref, wbi_ref, wcr_ref, wci_ref,
             dsk_ref, wg_ref, o_ref, hr_ref, hi_ref, tr_ref, ti_ref, start_states):
    chunk = u_ref.shape[1]
    seg = chunk // SUBLANES
    half_in = D_SSM // 2
    half_st = N_STATE // 2
    u = u_ref[0]
    ub = u.astype(BF16)
    for wref, href in ((wbr_ref, hr_ref), (wbi_ref, hi_ref)):
        href[:, :half_st] = jnp.dot(ub[:, :half_in], wref[:half_in, :half_st],
                                    preferred_element_type=F32)
        href[:, half_st:] = jnp.dot(ub[:, half_in:], wref[half_in:, half_st:],
                                    preferred_element_type=F32)

    ends_r, ends_i = [], []
    for cb in range(N_STATE // S5_COLS):
        cs = slice(cb * S5_COLS, (cb + 1) * S5_COLS)
        lr = jnp.broadcast_to(lbr_ref[:, cs], (SUBLANES, S5_COLS))
        li = jnp.broadcast_to(lbi_ref[:, cs], (SUBLANES, S5_COLS))

        def scan_step(j, carry):
            sr, si = carry
            rows = pl.ds(pl.multiple_of(j * SUBLANES, SUBLANES), SUBLANES)
            sr, si = _cmul_add(lr, li, sr, si, hr_ref[rows, cs], hi_ref[rows, cs])
            hr_ref[rows, cs] = sr
            hi_ref[rows, cs] = si
            return sr, si

        zero = jnp.zeros((SUBLANES, S5_COLS), F32)
        er, ei = lax.fori_loop(0, seg, scan_step, (zero, zero), unroll=min(seg, 4))
        ends_r.append(er)
        ends_i.append(ei)
    end_r = jnp.concatenate(ends_r, axis=1)
    end_i = jnp.concatenate(ends_i, axis=1)
    start_states(end_r, end_i)

    for cb in range(N_STATE // S5_COLS):
        cs = slice(cb * S5_COLS, (cb + 1) * S5_COLS)
        tr = tr_ref[:, cs]
        ti = ti_ref[:, cs]

        def fix_step(j, carry):
            rows = pl.ds(pl.multiple_of(j * SUBLANES, SUBLANES), SUBLANES)
            nr, ni = _cmul_add(pwr_ref[rows, cs], pwi_ref[rows, cs], tr, ti,
                               hr_ref[rows, cs], hi_ref[rows, cs])
            hr_ref[rows, cs] = nr
            hi_ref[rows, cs] = ni
            return carry

        lax.fori_loop(0, seg, fix_step, 0, unroll=min(seg, 4))

    half_out = D_SSM // 2
    ys = []
    for k in range(2):
        st = slice(k * half_st, (k + 1) * half_st)
        oc = slice(k * half_out, (k + 1) * half_out)
        ys.append(jnp.dot(hr_ref[:, st].astype(BF16), wcr_ref[st, oc], preferred_element_type=F32)
                  - jnp.dot(hi_ref[:, st].astype(BF16), wci_ref[st, oc], preferred_element_type=F32))
    y = jnp.concatenate(ys, axis=1) + dsk_ref[...] * u
    z = jax.nn.gelu(y, approximate=True)
    o_ref[0] = z * jax.nn.sigmoid(jnp.dot(z.astype(BF16), wg_ref[...], preferred_element_type=F32))


def _s5_seq_kernel(u_ref, lbr_ref, lbi_ref, pwr_ref, pwi_ref, wbr_ref, wbi_ref, wcr_ref, wci_ref,
                   dsk_ref, wg_ref, o_ref, hlr_ref, hli_ref,
                   hr_ref, hi_ref, tr_ref, ti_ref, cr_ref, ci_ref):
    chunk = u_ref.shape[1]

    @pl.when(pl.program_id(1) == 0)
    def _():
        cr_ref[...] = jnp.zeros_like(cr_ref)
        ci_ref[...] = jnp.zeros_like(ci_ref)

    def start_states(end_r, end_i):
        last = chunk - SUBLANES
        lsr = pwr_ref[last:last + 1, :]
        lsi = pwi_ref[last:last + 1, :]
        sr = cr_ref[...]
        si = ci_ref[...]
        for s in range(SUBLANES):
            tr_ref[s:s + 1, :] = sr
            ti_ref[s:s + 1, :] = si
            sr, si = _cmul_add(lsr, lsi, sr, si, end_r[s:s + 1, :], end_i[s:s + 1, :])
        cr_ref[...] = sr
        ci_ref[...] = si

    _s5_body(u_ref, lbr_ref, lbi_ref, pwr_ref, pwi_ref, wbr_ref, wbi_ref, wcr_ref, wci_ref,
             dsk_ref, wg_ref, o_ref, hr_ref, hi_ref, tr_ref, ti_ref, start_states)
    hlr_ref[0] = cr_ref[...]
    hli_ref[0] = ci_ref[...]


def _s5_batch_kernel(u_ref, h0r_ref, h0i_ref, lbr_ref, lbi_ref, pwr_ref, pwi_ref, wbr_ref, wbi_ref,
                     wcr_ref, wci_ref, dsk_ref, wg_ref, o_ref, hlr_ref, hli_ref,
                     hr_ref, hi_ref, tr_ref, ti_ref):
    chunk = u_ref.shape[1]

    def start_states(end_r, end_i):
        last = chunk - SUBLANES
        tr = h0r_ref[0]
        ti = h0i_ref[0]
        tr_ref[...] = tr
        ti_ref[...] = ti
        fr, fi = _cmul_add(pwr_ref[last:last + SUBLANES, :], pwi_ref[last:last + SUBLANES, :],
                           tr, ti, end_r, end_i)
        hlr_ref[0] = fr
        hli_ref[0] = fi

    _s5_body(u_ref, lbr_ref, lbi_ref, pwr_ref, pwi_ref, wbr_ref, wbi_ref, wcr_ref, wci_ref,
             dsk_ref, wg_ref, o_ref, hr_ref, hi_ref, tr_ref, ti_ref, start_states)


def _to_segment_rows(u, chunk):
    *lead, s, d = u.shape
    seg = chunk // SUBLANES
    x = u.reshape(*lead, s // chunk, SUBLANES, seg, d)
    return jnp.swapaxes(x, -2, -3).reshape(*lead, s, d)


def _from_segment_rows(u, chunk):
    *lead, s, d = u.shape
    seg = chunk // SUBLANES
    x = u.reshape(*lead, s // chunk, seg, SUBLANES, d)
    return jnp.swapaxes(x, -2, -3).reshape(*lead, s, d)


def _s5_scratch(chunk):
    return [pltpu.VMEM((chunk, N_STATE), F32), pltpu.VMEM((chunk, N_STATE), F32),
            pltpu.VMEM((SUBLANES, N_STATE), F32), pltpu.VMEM((SUBLANES, N_STATE), F32)]


def _s5_seq(u, prep, weights, chunk):
    n, s, d = u.shape
    lbr, lbi, _, _, pwr, pwi = prep
    consts = (lbr, lbi, pwr, pwi) + tuple(weights)
    state = jax.ShapeDtypeStruct((n, 1, N_STATE), F32)
    return pl.pallas_call(
        _s5_seq_kernel, name="s5_seq",
        grid=(n, s // chunk),
        in_specs=[pl.BlockSpec((1, chunk, d), lambda b, i: (b, i, 0))]
                 + [_const_spec(a.shape) for a in consts],
        out_specs=(pl.BlockSpec((1, chunk, d), lambda b, i: (b, i, 0)),
                   pl.BlockSpec((1, 1, N_STATE), lambda b, i: (b, 0, 0)),
                   pl.BlockSpec((1, 1, N_STATE), lambda b, i: (b, 0, 0))),
        out_shape=(jax.ShapeDtypeStruct((n, s, d), F32), state, state),
        scratch_shapes=_s5_scratch(chunk) + [pltpu.VMEM((1, N_STATE), F32), pltpu.VMEM((1, N_STATE), F32)],
        compiler_params=_cparams(("arbitrary", "arbitrary")),
    )(u, *consts)


def _s5_batch(u, h0r, h0i, prep, weights):
    nblk, chunk, d = u.shape
    lbr, lbi, _, _, pwr, pwi = prep
    consts = (lbr, lbi, pwr, pwi) + tuple(weights)
    state = jax.ShapeDtypeStruct((nblk, SUBLANES, N_STATE), F32)
    blk = lambda shape: pl.BlockSpec(shape, lambda i: (i, 0, 0))
    return pl.pallas_call(
        _s5_batch_kernel, name="s5_batch",
        grid=(nblk,),
        in_specs=[blk((1, chunk, d)), blk((1, SUBLANES, N_STATE)), blk((1, SUBLANES, N_STATE))]
                 + [_const_spec(a.shape) for a in consts],
        out_specs=(blk((1, chunk, d)), blk((1, SUBLANES, N_STATE)), blk((1, SUBLANES, N_STATE))),
        out_shape=(jax.ShapeDtypeStruct((nblk, chunk, d), F32), state, state),
        scratch_shapes=_s5_scratch(chunk),
        compiler_params=_cparams(("arbitrary",)),
    )(u, h0r, h0i, *consts)


_PEER_CELLS = [(i, j) for i in range(PEER_TOPK) for j in range(PEER_TOPK)
               if (i + 1) * (j + 1) <= PEER_TOPK]
_N_CELLS_PAD = -(-len(_PEER_CELLS) // SUBLANES) * SUBLANES
_BIG_ID = 1 << 20


def _peer_route_kernel(xn_ref, wq_ref, keys_ref, cell_ref, a_ref, b_ref, w_ref,
                       q_ref, sv_ref, si_ref, cand_ref, cv_ref, fid_ref, ao_ref, bo_ref, wo_ref):
    t = xn_ref.shape[0]
    q_ref[...] = jnp.dot(xn_ref[...], wq_ref[...], preferred_element_type=F32)
    key_row = lax.broadcasted_iota(jnp.int32, (N_KEYS, t), 0)

    def sub_topk(ht, carry):
        qh = q_ref[:, pl.ds(pl.multiple_of(ht * D_HALF, D_HALF), D_HALF)].astype(BF16)
        s = lax.dot_general(keys_ref[ht], qh, (((1,), (1,)), ((), ())),
                            preferred_element_type=F32)
        for r in range(PEER_TOPK):
            m = jnp.max(s, axis=0, keepdims=True)
            idx = jnp.min(jnp.where(s == m, key_row, N_KEYS), axis=0, keepdims=True)
            s = jnp.where(key_row == idx, NEG_INF, s)
            sv_ref[ht, r:r + 1, :] = m
            si_ref[ht, r:r + 1, :] = idx
        return carry

    lax.fori_loop(0, 2 * PEER_HEADS, sub_topk, 0)

    cell_row = lax.broadcasted_iota(jnp.int32, (_N_CELLS_PAD, t), 0)
    cell_id = cell_ref[...]

    def combine(h, carry):
        va = sv_ref[2 * h]
        vb = sv_ref[2 * h + 1]
        cand_ref[...] = jnp.full((_N_CELLS_PAD, t), NEG_INF, F32)
        for c, (i, j) in enumerate(_PEER_CELLS):
            cand_ref[c:c + 1, :] = va[i:i + 1, :] + vb[j:j + 1, :]
        cand = cand_ref[...]
        for r in range(PEER_TOPK):
            m = jnp.max(cand, axis=0, keepdims=True)
            ridx = jnp.min(jnp.where(cand == m, cell_row, _N_CELLS_PAD), axis=0, keepdims=True)
            hit = cell_row == ridx
            cand = jnp.where(hit, NEG_INF, cand)
            cv_ref[r:r + 1, :] = m
            fid_ref[r:r + 1, :] = jnp.max(jnp.where(hit, cell_id, 0), axis=0, keepdims=True)
        cv = cv_ref[...]
        fid = fid_ref[...]
        isel = lax.shift_right_logical(fid, 4)
        jsel = jnp.bitwise_and(fid, PEER_TOPK - 1)
        ia = si_ref[2 * h]
        ib = si_ref[2 * h + 1]
        ka = jnp.zeros((PEER_TOPK, t), jnp.int32)
        kb = jnp.zeros((PEER_TOPK, t), jnp.int32)
        for r in range(PEER_TOPK):
            ka = jnp.where(isel == r, ia[r:r + 1, :], ka)
            kb = jnp.where(jsel == r, ib[r:r + 1, :], kb)
        e = jnp.exp(cv - cv[0:1, :])
        ao_ref[h] = ka
        bo_ref[h] = kb
        wo_ref[h] = e / jnp.sum(e, axis=0, keepdims=True)
        return carry

    lax.fori_loop(0, PEER_HEADS, combine, 0)
    nk = PEER_HEADS * PEER_TOPK
    a_ref[...] = ao_ref[...].reshape(nk, t).T
    b_ref[...] = bo_ref[...].reshape(nk, t).T
    w_ref[...] = wo_ref[...].reshape(nk, t).T


def _peer_route(xn, wq, keys, t):
    n, d = xn.shape
    nk = PEER_HEADS * PEER_TOPK
    ids = [i * PEER_TOPK + j for i, j in _PEER_CELLS] + [_BIG_ID] * (_N_CELLS_PAD - len(_PEER_CELLS))
    cell = jnp.broadcast_to(jnp.asarray(ids, jnp.int32)[:, None], (_N_CELLS_PAD, t))
    tok = pl.BlockSpec((t, nk), lambda i: (i, 0))
    return pl.pallas_call(
        _peer_route_kernel, name="peer_route",
        grid=(n // t,),
        in_specs=[pl.BlockSpec((t, d), lambda i: (i, 0)), _const_spec(wq.shape),
                  _const_spec(keys.shape), _const_spec(cell.shape)],
        out_specs=(tok, tok, tok),
        out_shape=(jax.ShapeDtypeStruct((n, nk), jnp.int32), jax.ShapeDtypeStruct((n, nk), jnp.int32),
                   jax.ShapeDtypeStruct((n, nk), F32)),
        scratch_shapes=[pltpu.VMEM((t, wq.shape[1]), F32),
                        pltpu.VMEM((2 * PEER_HEADS, PEER_TOPK, t), F32),
                        pltpu.VMEM((2 * PEER_HEADS, PEER_TOPK, t), jnp.int32),
                        pltpu.VMEM((_N_CELLS_PAD, t), F32),
                        pltpu.VMEM((PEER_TOPK, t), F32), pltpu.VMEM((PEER_TOPK, t), jnp.int32),
                        pltpu.VMEM((PEER_HEADS, PEER_TOPK, t), jnp.int32),
                        pltpu.VMEM((PEER_HEADS, PEER_TOPK, t), jnp.int32),
                        pltpu.VMEM((PEER_HEADS, PEER_TOPK, t), F32)],
        compiler_params=_cparams(("arbitrary",)),
    )(xn, wq, keys, cell)


EXPERT_BLOCK = 2048
COEF_PITCH = 72
HALF_GROUPS = N_KEYS // 2


def _peer_hidden_kernel(xn_ref, a_ref, b_ref, w_ref, u_ref, c_ref, hsel_ref):
    j = pl.program_id(1)
    groups = u_ref.shape[0] // N_KEYS

    @pl.when(j == 0)
    def _():
        hsel_ref[...] = jnp.zeros_like(hsel_ref)

    h = lax.dot_general(xn_ref[...], u_ref[...], (((1,), (1,)), ((), ())),
                        preferred_element_type=F32)
    a_idx = a_ref[...]
    b_idx = b_ref[...]
    hs = hsel_ref[...]
    for g in range(groups):
        hg = h[:, g * N_KEYS:(g + 1) * N_KEYS]
        hs = jnp.where(a_idx == j * groups + g, jnp.take_along_axis(hg, b_idx, axis=1), hs)
    hsel_ref[...] = hs

    @pl.when(j == pl.num_programs(1) - 1)
    def _():
        c_ref[...] = w_ref[...] * jax.nn.gelu(hs, approximate=True)


def _peer_hidden(xn, a_idx, b_idx, w, u_bf16, t):
    n, d = xn.shape
    nk = a_idx.shape[1]
    ne = u_bf16.shape[0]
    tok = lambda width: pl.BlockSpec((t, width), lambda i, j: (i, 0))
    return pl.pallas_call(
        _peer_hidden_kernel, name="peer_hidden",
        grid=(n // t, ne // EXPERT_BLOCK),
        in_specs=[tok(d), tok(nk), tok(nk), tok(nk),
                  pl.BlockSpec((EXPERT_BLOCK, d), lambda i, j: (j, 0))],
        out_specs=tok(nk),
        out_shape=jax.ShapeDtypeStruct((n, nk), F32),
        scratch_shapes=[pltpu.VMEM((t, nk), F32)],
        compiler_params=_cparams(("arbitrary", "arbitrary")),
    )(xn, a_idx, b_idx, w, u_bf16)


def _peer_out_kernel(a_ref, b_ref, c_ref, x1_ref, v_ref, y_ref, s_ref, acc_ref):
    j = pl.program_id(1)
    t = a_ref.shape[0]
    pairs = v_ref.shape[0] // (2 * N_KEYS)
    hi_mask = jnp.uint32(0xFFFF0000)

    @pl.when(j == 0)
    def _():
        sub = lax.broadcasted_iota(jnp.int32, (N_KEYS, N_KEYS), 0)

        def one_token(i, carry):
            arow = a_ref[pl.ds(i, 1), :]
            brow = b_ref[pl.ds(i, 1), :]
            crow = c_ref[pl.ds(i, 1), :]
            pt = jnp.where(sub == arow, 1.0, 0.0).astype(BF16)
            rt = jnp.where(sub == brow, crow, 0.0).astype(BF16)
            coef = lax.dot_general(pt, rt, (((1,), (1,)), ((), ())), preferred_element_type=F32)
            bits = pltpu.bitcast(coef.astype(BF16).astype(F32), jnp.uint32)
            packed = jnp.bitwise_or(jnp.bitwise_and(bits[:HALF_GROUPS], hi_mask),
                                    lax.shift_right_logical(bits[HALF_GROUPS:], jnp.uint32(16)))
            s_ref[pl.ds(pl.multiple_of(i * COEF_PITCH, SUBLANES), HALF_GROUPS), :] = packed
            return carry

        lax.fori_loop(0, t, one_token, 0, unroll=32)

    parts = []
    for g in range(pairs):
        w32 = s_ref[pl.ds(j * pairs + g, t, stride=COEF_PITCH), :]
        parts.append(pltpu.bitcast(jnp.bitwise_and(w32, hi_mask), F32).astype(BF16))
        parts.append(pltpu.bitcast(lax.shift_left(w32, jnp.uint32(16)), F32).astype(BF16))
    coef_blk = jnp.concatenate(parts, axis=1)
    contrib = jnp.dot(coef_blk, v_ref[...], preferred_element_type=F32)

    @pl.when(j == 0)
    def _():
        acc_ref[...] = contrib

    @pl.when(j > 0)
    def _():
        acc_ref[...] += contrib

    @pl.when(j == pl.num_programs(1) - 1)
    def _():
        y_ref[...] = x1_ref[...] + acc_ref[...]


def _pair_groups(table_bf16):
    ne, d = table_bf16.shape
    x = table_bf16.reshape(2, HALF_GROUPS, N_KEYS, d)
    return jnp.swapaxes(x, 0, 1).reshape(ne, d)


def _peer_out(a_idx, b_idx, c, x1, v_paired, t):
    n, d = x1.shape
    nk = a_idx.shape[1]
    ne = v_paired.shape[0]
    tok = lambda width: pl.BlockSpec((t, width), lambda i, j: (i, 0))
    return pl.pallas_call(
        _peer_out_kernel, name="peer_out",
        grid=(n // t, ne // EXPERT_BLOCK),
        in_specs=[tok(nk), tok(nk), tok(nk), tok(d),
                  pl.BlockSpec((EXPERT_BLOCK, d), lambda i, j: (j, 0))],
        out_specs=tok(d),
        out_shape=jax.ShapeDtypeStruct((n, d), F32),
        scratch_shapes=[pltpu.VMEM((t * COEF_PITCH, N_KEYS), jnp.uint32), pltpu.VMEM((t, d), F32)],
        compiler_params=_cparams(("arbitrary", "arbitrary")),
    )(a_idx, b_idx, c, x1, v_paired)


TOKEN_TILE = 512
ATTN_TILE = 1024
S5_CHUNK = 256
ROUTE_TILE = 256
EXPERT_TILE = 512


def _tile(n, pref):
    t = min(pref, n)
    while n % t:
        t //= 2
    return t


def _tail(x1, xn2, lw):
    n = x1.shape[0]
    a_idx, b_idx, w = _peer_route(xn2, lw['wq'], lw['keys'], _tile(n, ROUTE_TILE))
    te = _tile(n, EXPERT_TILE)
    c = _peer_hidden(xn2, a_idx, b_idx, w, lw['u'], te)
    return _peer_out(a_idx, b_idx, c, x1, lw['v'], te)


def _layer_weights(l, norm1_g, w_in, b_forget, q_norm_g, k_norm_g, ssm_lam_re, ssm_lam_im, ssm_log_dt,
                   ssm_b_re, ssm_b_im, ssm_c_re, ssm_c_im, ssm_d, w_glu, w_proj_attn, w_proj_ssm, w_out,
                   norm2_g, w_query, sub_keys, expert_u, expert_v, chunks):
    lw = {'inproj': _prep_inproj(norm1_g[l], w_in[l], b_forget[l], q_norm_g[l], k_norm_g[l])}
    lw['s5_prep'] = {c: _s5_prep(ssm_lam_re[l], ssm_lam_im[l], ssm_log_dt[l], ssm_b_re[l], ssm_b_im[l], c)
                     for c in chunks}
    p0 = lw['s5_prep'][chunks[0]]
    lw['s5_w'] = _s5_weights(p0[2], p0[3], ssm_c_re[l], ssm_c_im[l], ssm_d[l], w_glu[l])
    lw['mix'] = (w_proj_attn[l].astype(BF16), w_proj_ssm[l].astype(BF16), w_out[l].astype(BF16),
                 norm2_g[l].astype(F32)[None, :])
    lw['wq'] = w_query[l].astype(BF16)
    lw['keys'] = sub_keys[l].reshape(2 * PEER_HEADS, N_KEYS, D_HALF).astype(BF16)
    lw['u'] = expert_u[l].astype(BF16)
    lw['v'] = _pair_groups(expert_v[l].astype(BF16))
    return lw


def kernel(x_prompt, x_sample, cache_k, cache_v, cache_logf, state_ssm_re, state_ssm_im, page_table,
           norm1_g, w_in, b_forget, q_norm_g, k_norm_g,
           ssm_lam_re, ssm_lam_im, ssm_log_dt, ssm_b_re, ssm_b_im, ssm_c_re, ssm_c_im, ssm_d, w_glu,
           w_proj_attn, w_proj_ssm, w_out, norm2_g, w_query, sub_keys, expert_u, expert_v):
    nb, s, d = x_prompt.shape
    db, t, _ = x_sample.shape
    depth = w_in.shape[0]
    assert db % SUBLANES == 0 and t * N_HEADS <= LANES and t <= PAGE
    batch_chunk = SUBLANES * t
    y_p = x_prompt.reshape(nb * s, d)
    y_s = x_sample.reshape(db * t, d)
    outs = [[] for _ in range(10)]
    for l in range(depth):
        lw = _layer_weights(l, norm1_g, w_in, b_forget, q_norm_g, k_norm_g, ssm_lam_re, ssm_lam_im,
                            ssm_log_dt, ssm_b_re, ssm_b_im, ssm_c_re, ssm_c_im, ssm_d, w_glu,
                            w_proj_attn, w_proj_ssm, w_out, norm2_g, w_query, sub_keys, expert_u,
                            expert_v, (S5_CHUNK, batch_chunk))

        tm = _tile(nb * s, TOKEN_TILE)
        qb, kb, vb, kf, vf, lf, u, ga, gs = _inproj(y_p, *lw['inproj'], tm=tm)
        lf3 = lf.reshape(nb, s, N_HEADS)
        ta = _tile(s, ATTN_TILE)
        ct = _cumsum(jnp.swapaxes(lf3, 1, 2), ta)
        o_attn = _attn_prompt(qb.reshape(nb, s, D_ATTN), kb.reshape(nb, s, D_ATTN),
                              vb.reshape(nb, s, D_ATTN), ct, ta)
        u_seg = _to_segment_rows(u.reshape(nb, s, D_SSM), S5_CHUNK)
        o_ssm, hr, hi = _s5_seq(u_seg, lw['s5_prep'][S5_CHUNK], lw['s5_w'], S5_CHUNK)
        o_ssm = _from_segment_rows(o_ssm, S5_CHUNK).reshape(nb * s, D_SSM)
        x1, xn2 = _mix(o_attn.reshape(nb * s, D_ATTN), o_ssm, ga, gs, y_p, *lw['mix'], tm=tm)
        y_p = _tail(x1, xn2, lw)
        for slot, val in zip(range(5), (kf.reshape(nb, s, N_HEADS, HEAD_DIM),
                                        vf.reshape(nb, s, N_HEADS, HEAD_DIM), lf3,
                                        hr.reshape(nb, N_GROUPS, STATE_DIM),
                                        hi.reshape(nb, N_GROUPS, STATE_DIM))):
            outs[slot].append(val)

        tm = _tile(db * t, TOKEN_TILE)
        qb, kb, vb, kf, vf, lf, u, ga, gs = _inproj(y_s, *lw['inproj'], tm=tm)
        o_attn = _attn_paged(page_table, qb, kf, vf, lf, cache_k, cache_v, cache_logf, t, l).astype(BF16)
        nblk = db // SUBLANES
        u_seg = _to_segment_rows(u.reshape(nblk, batch_chunk, D_SSM), batch_chunk)
        h0r = state_ssm_re[l].astype(F32).reshape(nblk, SUBLANES, N_STATE)
        h0i = state_ssm_im[l].astype(F32).reshape(nblk, SUBLANES, N_STATE)
        o_ssm, hr, hi = _s5_batch(u_seg, h0r, h0i, lw['s5_prep'][batch_chunk], lw['s5_w'])
        o_ssm = _from_segment_rows(o_ssm, batch_chunk).reshape(db * t, D_SSM)
        x1, xn2 = _mix(o_attn, o_ssm, ga, gs, y_s, *lw['mix'], tm=tm)
        y_s = _tail(x1, xn2, lw)
        for slot, val in zip(range(5, 10), (kf.reshape(db, t, N_HEADS, HEAD_DIM),
                                            vf.reshape(db, t, N_HEADS, HEAD_DIM),
                                            lf.reshape(db, t, N_HEADS),
                                            hr.reshape(db, N_GROUPS, STATE_DIM),
                                            hi.reshape(db, N_GROUPS, STATE_DIM))):
            outs[slot].append(val)
    stacked = [jnp.stack(o) for o in outs]
    return (y_p.reshape(nb, s, d), y_s.reshape(db, t, d), *stacked)
```

```python
import functools
import math

import jax
import jax.numpy as jnp
from jax import lax
from jax.experimental import pallas as pl
from jax.experimental.pallas import tpu as pltpu

F32 = jnp.float32
BF16 = jnp.bfloat16

N_HEADS = 8
HEAD_DIM = 64
D_ATTN = N_HEADS * HEAD_DIM
ATTN_SCALE = HEAD_DIM ** -0.5
LOG2E = math.log2(math.e)
D_SSM = 512
SSM_GROUP = 16
N_GROUPS = D_SSM // SSM_GROUP
STATE_DIM = 64
N_STATE = N_GROUPS * STATE_DIM
PEER_HEADS = 8
N_KEYS = 128
PEER_TOPK = 16
D_HALF = 128
NORM_EPS = 1e-6
NEG_INF = -1e30
PAGE = 128

LANES = 128
SUBLANES = 8
VMEM_LIMIT = 56 * 1024 * 1024


def _cparams(sem):
    return pltpu.CompilerParams(dimension_semantics=sem, vmem_limit_bytes=VMEM_LIMIT)


def _const_spec(shape):
    nd = len(shape)
    return pl.BlockSpec(shape, lambda *_: (0,) * nd)


def _rms_rows(x, g):
    r = lax.rsqrt(jnp.mean(x * x, axis=-1, keepdims=True) + NORM_EPS)
    return (x * r) * g


def _split_dot(a, b_bf16):
    hi = a.astype(BF16)
    lo = (a - hi.astype(F32)).astype(BF16)
    return (jnp.dot(hi, b_bf16, preferred_element_type=F32)
            + jnp.dot(lo, b_bf16, preferred_element_type=F32))


def _log_sigmoid(x):
    return jnp.minimum(x, 0.0) - jnp.log1p(jnp.exp(-jnp.abs(x)))


def _inproj_kernel(x_ref, g1_ref, w_ref, wf_ref, bf_ref, gq_ref, gk_ref, hsel_ref,
                   qb_ref, kb_ref, vb_ref, kf_ref, vf_ref, lf_ref, u_ref, ga_ref, gs_ref):
    xn = _rms_rows(x_ref[...], g1_ref[...]).astype(BF16)

    def seg(i, n):
        return jnp.dot(xn, w_ref[:, i:i + n], preferred_element_type=F32)

    hsel = hsel_ref[...]

    def head_norm(z, g):
        ms = _split_dot(z * z, hsel)
        return (z * lax.rsqrt(ms + NORM_EPS)) * g

    q = head_norm(seg(0, D_ATTN), gq_ref[...])
    qb_ref[...] = (q * (ATTN_SCALE * LOG2E)).astype(BF16)
    k = head_norm(seg(D_ATTN, D_ATTN), gk_ref[...])
    kf_ref[...] = k
    kb_ref[...] = k.astype(BF16)
    v = seg(2 * D_ATTN, D_ATTN)
    vf_ref[...] = v
    vb_ref[...] = v.astype(BF16)
    u_ref[...] = seg(3 * D_ATTN, D_SSM)
    o = 3 * D_ATTN + D_SSM
    d = ga_ref.shape[1]
    ga_ref[...] = jax.nn.sigmoid(seg(o, d)).astype(BF16)
    gs_ref[...] = jax.nn.sigmoid(seg(o + d, d)).astype(BF16)
    zf = jnp.dot(xn, wf_ref[...], preferred_element_type=F32) + bf_ref[...]
    lf_ref[...] = _log_sigmoid(zf)[:, :N_HEADS]


def _prep_inproj(norm1_g, w_in, b_forget, q_norm_g, k_norm_g):
    o3 = 3 * D_ATTN
    o4 = o3 + N_HEADS
    wmain = jnp.concatenate([w_in[:, :o3], w_in[:, o4:]], axis=1).astype(BF16)
    wf = jnp.pad(w_in[:, o3:o4], ((0, 0), (0, LANES - N_HEADS))).astype(BF16)
    bfp = jnp.pad(b_forget.astype(F32), (0, LANES - N_HEADS))[None, :]
    gq = jnp.tile(q_norm_g.astype(F32), N_HEADS)[None, :]
    gk = jnp.tile(k_norm_g.astype(F32), N_HEADS)[None, :]
    hsel = jnp.kron(jnp.eye(N_HEADS, dtype=F32),
                    jnp.full((HEAD_DIM, HEAD_DIM), 1.0 / HEAD_DIM, F32)).astype(BF16)
    return norm1_g.astype(F32)[None, :], wmain, wf, bfp, gq, gk, hsel


def _inproj(x, g1, wmain, wf, bfp, gq, gk, hsel, tm):
    n, d = x.shape
    tok = lambda w: pl.BlockSpec((tm, w), lambda i: (i, 0))
    out_shape = (
        jax.ShapeDtypeStruct((n, D_ATTN), BF16), jax.ShapeDtypeStruct((n, D_ATTN), BF16),
        jax.ShapeDtypeStruct((n, D_ATTN), BF16), jax.ShapeDtypeStruct((n, D_ATTN), F32),
        jax.ShapeDtypeStruct((n, D_ATTN), F32), jax.ShapeDtypeStruct((n, N_HEADS), F32),
        jax.ShapeDtypeStruct((n, D_SSM), F32), jax.ShapeDtypeStruct((n, d), BF16),
        jax.ShapeDtypeStruct((n, d), BF16))
    return pl.pallas_call(
        _inproj_kernel, name="inproj",
        grid=(n // tm,),
        in_specs=[tok(d), _const_spec(g1.shape), _const_spec(wmain.shape), _const_spec(wf.shape),
                  _const_spec(bfp.shape), _const_spec(gq.shape), _const_spec(gk.shape),
                  _const_spec(hsel.shape)],
        out_specs=(tok(D_ATTN), tok(D_ATTN), tok(D_ATTN), tok(D_ATTN), tok(D_ATTN), tok(N_HEADS),
                   tok(D_SSM), tok(d), tok(d)),
        out_shape=out_shape,
        compiler_params=_cparams(("arbitrary",)),
    )(x, g1, wmain, wf, bfp, gq, gk, hsel)


def _mix_kernel(oa_ref, os_ref, ga_ref, gs_ref, x_ref, wpa_ref, wps_ref, wo_ref, g2_ref,
                x1_ref, xn_ref):
    pa = jnp.dot(oa_ref[...], wpa_ref[...], preferred_element_type=F32)
    ps = jnp.dot(os_ref[...].astype(BF16), wps_ref[...], preferred_element_type=F32)
    mixed = ga_ref[...].astype(F32) * pa + gs_ref[...].astype(F32) * ps
    x1 = x_ref[...] + jnp.dot(mixed.astype(BF16), wo_ref[...], preferred_element_type=F32)
    x1_ref[...] = x1
    xn_ref[...] = _rms_rows(x1, g2_ref[...]).astype(BF16)


def _mix(oa, os_, ga, gs, x, wpa, wps, wo, g2, tm):
    n, d = x.shape
    tok = lambda w: pl.BlockSpec((tm, w), lambda i: (i, 0))
    return pl.pallas_call(
        _mix_kernel, name="mix",
        grid=(n // tm,),
        in_specs=[tok(D_ATTN), tok(D_SSM), tok(d), tok(d), tok(d), _const_spec(wpa.shape),
                  _const_spec(wps.shape), _const_spec(wo.shape), _const_spec(g2.shape)],
        out_specs=(tok(d), tok(d)),
        out_shape=(jax.ShapeDtypeStruct((n, d), F32), jax.ShapeDtypeStruct((n, d), BF16)),
        compiler_params=_cparams(("arbitrary",)),
    )(oa, os_, ga, gs, x, wpa, wps, wo, g2)


def _split3(a):
    parts = []
    rem = a
    for _ in range(3):
        part = rem.astype(BF16)
        rem = rem - part.astype(F32)
        parts.append(part)
    return parts


def _dot01_right(a, b01):
    return sum(jnp.dot(p, b01, preferred_element_type=F32) for p in _split3(a))


def _cumsum_kernel(lft_ref, ct_ref, carry_ref):
    t = lft_ref.shape[2]

    @pl.when(pl.program_id(1) == 0)
    def _():
        carry_ref[...] = jnp.zeros_like(carry_ref)

    r = lax.broadcasted_iota(jnp.int32, (t, t), 0)
    c = lax.broadcasted_iota(jnp.int32, (t, t), 1)
    upto = jnp.where(r <= c, 1.0, 0.0).astype(BF16)
    cs = carry_ref[...] + _dot01_right(lft_ref[0], upto)
    ct_ref[0] = cs * LOG2E
    carry_ref[...] = cs[:, t - 1:t]


def _cumsum(lft, t):
    n, h, s = lft.shape
    return pl.pallas_call(
        _cumsum_kernel, name="logf_cumsum",
        grid=(n, s // t),
        in_specs=[pl.BlockSpec((1, h, t), lambda b, i: (b, 0, i))],
        out_specs=pl.BlockSpec((1, h, t), lambda b, i: (b, 0, i)),
        out_shape=jax.ShapeDtypeStruct((n, h, s), F32),
        scratch_shapes=[pltpu.VMEM((h, 1), F32)],
        compiler_params=_cparams(("arbitrary", "arbitrary")),
    )(lft)


ATTN_ROW_BLOCK = 256


def _attn_prompt_kernel(first_ref, q_ref, k_ref, v_ref, ct_ref, o_ref, m_ref, acc_ref,
                        s0_ref, s1_ref, p0_ref, p1_ref, a0_ref, a1_ref):
    tq = q_ref.shape[1]
    tk = k_ref.shape[1]
    rb_rows = min(ATTN_ROW_BLOCK, tq)
    n_rb = tq // rb_rows
    n_iter = (N_HEADS // 2) * n_rb
    qi = pl.program_id(1)
    ki = first_ref[pl.program_id(0), qi] + pl.program_id(2)
    s_refs = (s0_ref, s1_ref)
    p_refs = (p0_ref, p1_ref)
    a_refs = (a0_ref, a1_ref)

    @pl.when(pl.program_id(2) == 0)
    def _():
        m_ref[...] = jnp.full_like(m_ref, NEG_INF)
        acc_ref[...] = jnp.zeros_like(acc_ref)

    lane = lax.broadcasted_iota(jnp.int32, (tk, LANES), 1)
    keep = (jnp.where(lane < HEAD_DIM, 1.0, 0.0).astype(BF16),
            jnp.where(lane < HEAD_DIM, 0.0, 1.0).astype(BF16))
    ones = (jnp.where(lane == HEAD_DIM, 1.0, 0.0).astype(BF16),
            jnp.where(lane == 0, 1.0, 0.0).astype(BF16))

    def where_is(j):
        pair = j // n_rb
        return pair, pl.multiple_of((j - pair * n_rb) * rb_rows, rb_rows)

    def slab(pair):
        return pl.ds(pl.multiple_of(pair * LANES, LANES), LANES)

    def scores(j, half):
        pair, row0 = where_is(j)
        qblk = q_ref[0, pl.ds(row0, rb_rows), slab(pair)]
        kh = k_ref[0, :, slab(pair)] * keep[half]
        s_refs[half][...] = lax.dot_general(qblk, kh, (((1,), (1,)), ((), ())),
                                            preferred_element_type=F32)

    def softmax(j, half, diagonal):
        pair, row0 = where_is(j)
        h = 2 * pair + half
        ckb = jnp.broadcast_to(ct_ref[0, pl.ds(h, 1), :], (SUBLANES, tk))
        if diagonal:
            col = lax.broadcasted_iota(jnp.int32, (SUBLANES, tk), 1)
            sub = lax.broadcasted_iota(jnp.int32, (SUBLANES, tk), 0) + row0
        probs = None
        for g in range(rb_rows // SUBLANES):
            lo = g * SUBLANES
            s = s_refs[half][lo:lo + SUBLANES, :] - ckb
            if diagonal:
                s = jnp.where(col <= sub + lo, s, NEG_INF)
            rows = pl.ds(row0 + lo, SUBLANES)
            m_prev = m_ref[h, rows, :]
            m_new = jnp.maximum(m_prev, jnp.max(s, axis=-1, keepdims=True))
            a_refs[half][lo:lo + SUBLANES, :] = jnp.exp2(m_prev - m_new)
            m_ref[h, rows, :] = m_new
            prob = jnp.exp2(s - m_new)
            if g % 2 == 0:
                probs = prob
            else:
                p_refs[half][lo - SUBLANES:lo + SUBLANES, :] = (
                    jnp.concatenate([probs, prob], axis=0).astype(BF16))

    def values(j, half):
        pair, row0 = where_is(j)
        h = 2 * pair + half
        vh = v_ref[0, :, slab(pair)] * keep[half] + ones[half]
        pv = jnp.dot(p_refs[half][...], vh, preferred_element_type=F32)
        rows = pl.ds(row0, rb_rows)
        acc_ref[h, rows, :] = a_refs[half][...] * acc_ref[h, rows, :] + pv

    def step(diagonal):
        scores(0, 0)
        scores(0, 1)
        softmax(0, 0, diagonal)

        def body(j, carry):
            scores(j, 0)
            softmax(j - 1, 1, diagonal)
            values(j - 1, 0)
            scores(j, 1)
            softmax(j, 0, diagonal)
            values(j - 1, 1)
            return carry

        lax.fori_loop(1, n_iter, body, 0)
        softmax(n_iter - 1, 1, diagonal)
        values(n_iter - 1, 0)
        values(n_iter - 1, 1)

    @pl.when(ki < qi)
    def _():
        step(False)

    @pl.when(ki == qi)
    def _():
        step(True)
        lane_q = lax.broadcasted_iota(jnp.int32, (tq, LANES), 1) < HEAD_DIM
        for hp in range(N_HEADS // 2):
            lo = acc_ref[2 * hp]
            hi = acc_ref[2 * hp + 1]
            out = jnp.where(lane_q, lo / lo[:, HEAD_DIM:HEAD_DIM + 1], hi / hi[:, 0:1])
            o_ref[0, :, hp * LANES:(hp + 1) * LANES] = out.astype(o_ref.dtype)


UNDERFLOW_LOG2 = 160.0


def _first_key_block(qb, kb, ct, t):
    n, s, d = qb.shape
    nb = s // t

    def block_norm(x):
        x2 = jnp.square(x.astype(F32)).reshape(n, nb, t, N_HEADS, HEAD_DIM).sum(-1)
        return jnp.sqrt(x2.max(axis=2)) * 1.02
    qn = block_norm(qb)
    kn = block_norm(kb)
    c_start = jnp.swapaxes(ct[:, :, 0::t], 1, 2)
    c_end = jnp.swapaxes(ct[:, :, t - 1::t], 1, 2)
    bound = (qn[:, :, None] * kn[:, None, :] + (qn * kn)[:, :, None]
             + c_start[:, :, None] - c_end[:, None, :])
    needed = jnp.any(bound >= -UNDERFLOW_LOG2, axis=-1)
    i_idx = jnp.arange(nb)[:, None]
    j_idx = jnp.arange(nb)[None, :]
    needed = jnp.where(j_idx < i_idx, needed, j_idx == i_idx)
    return jnp.argmax(needed, axis=-1).astype(jnp.int32)


def _attn_prompt(qb, kb, vb, ct, t):
    n, s, d = qb.shape
    nb = s // t
    rb = min(ATTN_ROW_BLOCK, t)
    first = _first_key_block(qb, kb, ct, t)
    qmap = lambda b, i, j, first: (b, i, 0)
    kblock = lambda b, i, j, first: jnp.minimum(first[b, i] + j, i)
    kmap = lambda b, i, j, first: (b, kblock(b, i, j, first), 0)
    grid_spec = pltpu.PrefetchScalarGridSpec(
        num_scalar_prefetch=1,
        grid=(n, nb, nb),
        in_specs=[pl.BlockSpec((1, t, d), qmap), pl.BlockSpec((1, t, d), kmap),
                  pl.BlockSpec((1, t, d), kmap),
                  pl.BlockSpec((1, N_HEADS, t), lambda b, i, j, first: (b, 0, kblock(b, i, j, first)))],
        out_specs=pl.BlockSpec((1, t, d), qmap),
        scratch_shapes=[pltpu.VMEM((N_HEADS, t, 1), F32), pltpu.VMEM((N_HEADS, t, LANES), F32),
                        pltpu.VMEM((rb, t), F32), pltpu.VMEM((rb, t), F32),
                        pltpu.VMEM((rb, t), BF16), pltpu.VMEM((rb, t), BF16),
                        pltpu.VMEM((rb, 1), F32), pltpu.VMEM((rb, 1), F32)])
    return pl.pallas_call(
        _attn_prompt_kernel, name="attn_prompt",
        grid_spec=grid_spec,
        out_shape=jax.ShapeDtypeStruct((n, s, d), BF16),
        compiler_params=_cparams(("arbitrary", "arbitrary", "arbitrary")),
    )(first, qb, kb, vb, ct)


PAGES_PER_STEP = 16


def _attn_paged_kernel(*refs, n_new, group):
    pt_ref, q_ref, kn_ref, vn_ref, lfn_ref = refs[:5]
    kc_refs = refs[5:5 + group]
    vc_refs = refs[5 + group:5 + 2 * group]
    lfc_refs = refs[5 + 2 * group:5 + 3 * group]
    o_ref, m_ref, l_ref, acc_ref, carry_ref, cq_ref = refs[5 + 3 * group:]
    del pt_ref
    p = pl.program_id(1)
    rows = n_new * N_HEADS
    krow = lax.broadcasted_iota(jnp.int32, (PAGE, PAGE), 0)
    kcol = lax.broadcasted_iota(jnp.int32, (PAGE, PAGE), 1)
    later = jnp.where(krow > kcol, 1.0, 0.0).astype(BF16)
    qbd = q_ref[0]

    def suffix_in_page(lf_page):
        loc = _dot01_right(lf_page, later)
        return loc, loc[:, 0:1] + lf_page[:, 0:1]

    def visit(pages, carry, valid):
        scores, vals = [], []
        for k_page, v_page, lf_page in pages:
            loc, total = suffix_in_page(lf_page)
            bias = jnp.concatenate([(loc + carry) * LOG2E] * n_new, axis=0)
            s = jnp.dot(qbd, k_page.astype(BF16), preferred_element_type=F32) + bias
            scores.append(s)
            vals.append(v_page.astype(BF16))
            carry = carry + total
        s = jnp.concatenate(scores, axis=1)
        if valid is not None:
            s = jnp.where(valid, s, NEG_INF)
        cq = cq_ref[...]
        m_prev = m_ref[...]
        m_new = jnp.maximum(m_prev, jnp.max(s, axis=1, keepdims=True) + cq)
        prob = jnp.exp2(s - (m_new - cq))
        alpha = jnp.exp2(m_prev - m_new)
        l_ref[...] = alpha * l_ref[...] + jnp.sum(prob, axis=1, keepdims=True)
        m_ref[...] = m_new
        pb = prob.astype(BF16)
        pv = None
        for i, v in enumerate(vals):
            t = lax.dot_general(pb[:, i * PAGE:(i + 1) * PAGE], v, (((1,), (1,)), ((), ())),
                                preferred_element_type=F32)
            pv = t if pv is None else pv + t
        acc_ref[...] = alpha * acc_ref[...] + pv
        return carry

    @pl.when(p == 0)
    def _():
        row = lax.broadcasted_iota(jnp.int32, (rows, PAGE), 0)
        col = lax.broadcasted_iota(jnp.int32, (rows, PAGE), 1)
        qry = lax.shift_right_logical(row, 3)
        lf_new = lfn_ref[0]
        loc, _ = suffix_in_page(lf_new)
        tiled = jnp.concatenate([loc] * n_new, axis=0)
        cq_ref[...] = -LOG2E * jnp.sum(jnp.where(col == qry, tiled, 0.0), axis=1, keepdims=True)
        m_ref[...] = jnp.full_like(m_ref, NEG_INF)
        l_ref[...] = jnp.zeros_like(l_ref)
        acc_ref[...] = jnp.zeros_like(acc_ref)
        carry_ref[...] = visit([(kn_ref[0], vn_ref[0], lf_new)], jnp.zeros((N_HEADS, 1), F32),
                               col <= qry)

    @pl.when(p > 0)
    def _():
        pages = [(kc_refs[i][0, 0], vc_refs[i][0, 0], lfc_refs[i][0, 0]) for i in range(group)]
        carry_ref[...] = visit(pages, carry_ref[...], None)

    @pl.when(p == pl.num_programs(1) - 1)
    def _():
        o_ref[0] = acc_ref[...] / l_ref[...]


def _attn_paged(page_table, qb, kf, vf, lf, cache_k, cache_v, cache_logf, n_new, layer):
    nb, n_pages = page_table.shape
    rows = n_new * N_HEADS
    group = _tile(n_pages, PAGES_PER_STEP)
    ck = jnp.transpose(cache_k, (0, 1, 3, 4, 2)).reshape(cache_k.shape[0], -1, D_ATTN, PAGE)
    cv = jnp.transpose(cache_v, (0, 1, 3, 4, 2)).reshape(cache_v.shape[0], -1, D_ATTN, PAGE)
    cl = jnp.transpose(cache_logf, (0, 1, 3, 2))
    q4 = qb.reshape(nb, n_new, N_HEADS, HEAD_DIM)
    qbd = jnp.einsum('bthd,gh->btghd', q4, jnp.eye(N_HEADS, dtype=qb.dtype)).reshape(nb, rows, D_ATTN)
    key_minor = lambda x: jnp.pad(jnp.swapaxes(x.reshape(nb, n_new, -1), 1, 2),
                                  ((0, 0), (0, 0), (0, PAGE - n_new)))
    per_seq = lambda b, p, pt: (b, 0, 0)

    def page(i):
        return lambda b, p, pt: (layer, pt[b, n_pages - 1 - (jnp.maximum(p, 1) - 1) * group - i], 0, 0)

    in_specs = [pl.BlockSpec((1, rows, D_ATTN), per_seq), pl.BlockSpec((1, D_ATTN, PAGE), per_seq),
                pl.BlockSpec((1, D_ATTN, PAGE), per_seq), pl.BlockSpec((1, N_HEADS, PAGE), per_seq)]
    in_specs += [pl.BlockSpec((1, 1, D_ATTN, PAGE), page(i)) for i in range(group)] * 2
    in_specs += [pl.BlockSpec((1, 1, N_HEADS, PAGE), page(i)) for i in range(group)]
    grid_spec = pltpu.PrefetchScalarGridSpec(
        num_scalar_prefetch=1,
        grid=(nb, n_pages // group + 1),
        in_specs=in_specs,
        out_specs=pl.BlockSpec((1, rows, D_ATTN), per_seq),
        scratch_shapes=[pltpu.VMEM((rows, 1), F32), pltpu.VMEM((rows, 1), F32),
                        pltpu.VMEM((rows, D_ATTN), F32), pltpu.VMEM((N_HEADS, 1), F32),
                        pltpu.VMEM((rows, 1), F32)])
    o = pl.pallas_call(
        functools.partial(_attn_paged_kernel, n_new=n_new, group=group), name="attn_paged",
        grid_spec=grid_spec,
        out_shape=jax.ShapeDtypeStruct((nb, rows, D_ATTN), F32),
        compiler_params=_cparams(("arbitrary", "arbitrary")),
    )(page_table, qbd, key_minor(kf), key_minor(vf), key_minor(lf),
      *([ck] * group), *([cv] * group), *([cl] * group))
    o5 = o.reshape(nb, n_new, N_HEADS, N_HEADS, HEAD_DIM)
    return jnp.einsum('btggd->btgd', o5).reshape(nb * n_new, D_ATTN)


S5_COLS = 512


def _s5_prep_kernel(lre_ref, lim_ref, ldt_ref, bre_ref, bim_ref, jp_ref,
                    lbr_ref, lbi_ref, bbr_ref, bbi_ref, pwr_ref, pwi_ref):
    lre = lre_ref[...]
    lim = lim_ref[...]
    dt = jnp.exp(ldt_ref[...])
    mag = jnp.exp(lre * dt)
    ang = lim * dt
    lbr = mag * jnp.cos(ang)
    lbi = mag * jnp.sin(ang)
    lbr_ref[...] = lbr
    lbi_ref[...] = lbi
    nr = lbr - 1.0
    den = lre * lre + lim * lim
    cr = (nr * lre + lbi * lim) / den
    ci = (lbi * lre - nr * lim) / den
    bre = bre_ref[...]
    bim = bim_ref[...]
    bbr_ref[...] = cr * bre - ci * bim
    bbi_ref[...] = cr * bim + ci * bre
    jp = jp_ref[...]
    pmag = jnp.exp(jp * (lre * dt))
    pang = jp * ang
    pwr_ref[...] = pmag * jnp.cos(pang)
    pwi_ref[...] = pmag * jnp.sin(pang)


def _s5_prep(lam_re, lam_im, log_dt, b_re, b_im, chunk):
    lre = lam_re.reshape(1, N_STATE).astype(F32)
    lim = lam_im.reshape(1, N_STATE).astype(F32)
    ldt = jnp.repeat(log_dt.astype(F32), STATE_DIM).reshape(1, N_STATE)
    bre = jnp.transpose(b_re.astype(F32), (2, 0, 1)).reshape(SSM_GROUP, N_STATE)
    bim = jnp.transpose(b_im.astype(F32), (2, 0, 1)).reshape(SSM_GROUP, N_STATE)
    jp = (jnp.arange(chunk, dtype=jnp.int32) // SUBLANES + 1).astype(F32).reshape(chunk, 1)
    vec = jax.ShapeDtypeStruct((1, N_STATE), F32)
    mat = jax.ShapeDtypeStruct((SSM_GROUP, N_STATE), F32)
    tab = jax.ShapeDtypeStruct((chunk, N_STATE), F32)
    return pl.pallas_call(
        _s5_prep_kernel, name="s5_prep",
        out_shape=(vec, vec, mat, mat, tab, tab),
        compiler_params=pltpu.CompilerParams(vmem_limit_bytes=VMEM_LIMIT),
    )(lre, lim, ldt, bre, bim, jp)


def _s5_weights(bbr, bbi, c_re, c_im, d_skip, w_glu):
    eye = jnp.eye(N_GROUPS, dtype=F32)

    def in_map(bb):
        b3 = bb.reshape(SSM_GROUP, N_GROUPS, STATE_DIM)
        return jnp.einsum('cgp,gh->gchp', b3, eye).reshape(D_SSM, N_STATE).astype(BF16)

    def out_map(cc):
        return jnp.einsum('gcp,gh->gphc', cc.astype(F32), eye).reshape(N_STATE, D_SSM).astype(BF16)

    return (in_map(bbr), in_map(bbi), out_map(c_re), out_map(c_im),
            d_skip.astype(F32).reshape(1, D_SSM), w_glu.astype(BF16))


def _cmul_add(ar, ai, br, bi, cr, ci):
    return ar * br - ai * bi + cr, ar * bi + ai * br + ci


def _s5_body(u_ref, lbr_ref, lbi_ref, pwr_ref, pwi_ref, wbr_ref, wbi_ref, wcr_ref, wci_ref,
             dsk_ref, wg_ref, o_ref, hr_ref, hi_ref, tr_ref, ti_ref, start_states):
    chunk = u_ref.shape[1]
    seg = chunk // SUBLANES
    half_in = D_SSM // 2
    half_st = N_STATE // 2
    u = u_ref[0]
    ub = u.astype(BF16)
    for wref, href in ((wbr_ref, hr_ref), (wbi_ref, hi_ref)):
        href[:, :half_st] = jnp.dot(ub[:, :half_in], wref[:half_in, :half_st],
                                    preferred_element_type=F32)
        href[:, half_st:] = jnp.dot(ub[:, half_in:], wref[half_in:, half_st:],
                                    preferred_element_type=F32)

    ends_r, ends_i = [], []
    for cb in range(N_STATE // S5_COLS):
        cs = slice(cb * S5_COLS, (cb + 1) * S5_COLS)
        lr = jnp.broadcast_to(lbr_ref[:, cs], (SUBLANES, S5_COLS))
        li = jnp.broadcast_to(lbi_ref[:, cs], (SUBLANES, S5_COLS))

        def scan_step(j, carry):
            sr, si = carry
            rows = pl.ds(pl.multiple_of(j * SUBLANES, SUBLANES), SUBLANES)
            sr, si = _cmul_add(lr, li, sr, si, hr_ref[rows, cs], hi_ref[rows, cs])
            hr_ref[rows, cs] = sr
            hi_ref[rows, cs] = si
            return sr, si

        zero = jnp.zeros((SUBLANES, S5_COLS), F32)
        er, ei = lax.fori_loop(0, seg, scan_step, (zero, zero), unroll=min(seg, 4))
        ends_r.append(er)
        ends_i.append(ei)
    end_r = jnp.concatenate(ends_r, axis=1)
    end_i = jnp.concatenate(ends_i, axis=1)
    start_states(end_r, end_i)

    for cb in range(N_STATE // S5_COLS):
        cs = slice(cb * S5_COLS, (cb + 1) * S5_COLS)
        tr = tr_ref[:, cs]
        ti = ti_ref[:, cs]

        def fix_step(j, carry):
            rows = pl.ds(pl.multiple_of(j * SUBLANES, SUBLANES), SUBLANES)
            nr, ni = _cmul_add(pwr_ref[rows, cs], pwi_ref[rows, cs], tr, ti,
                               hr_ref[rows, cs], hi_ref[rows, cs])
            hr_ref[rows, cs] = nr
            hi_ref[rows, cs] = ni
            return carry

        lax.fori_loop(0, seg, fix_step, 0, unroll=min(seg, 4))

    half_out = D_SSM // 2
    ys = []
    for k in range(2):
        st = slice(k * half_st, (k + 1) * half_st)
        oc = slice(k * half_out, (k + 1) * half_out)
        ys.append(jnp.dot(hr_ref[:, st].astype(BF16), wcr_ref[st, oc], preferred_element_type=F32)
                  - jnp.dot(hi_ref[:, st].astype(BF16), wci_ref[st, oc], preferred_element_type=F32))
    y = jnp.concatenate(ys, axis=1) + dsk_ref[...] * u
    z = jax.nn.gelu(y, approximate=True)
    o_ref[0] = z * jax.nn.sigmoid(jnp.dot(z.astype(BF16), wg_ref[...], preferred_element_type=F32))


def _s5_seq_kernel(u_ref, lbr_ref, lbi_ref, pwr_ref, pwi_ref, wbr_ref, wbi_ref, wcr_ref, wci_ref,
                   dsk_ref, wg_ref, o_ref, hlr_ref, hli_ref,
                   hr_ref, hi_ref, tr_ref, ti_ref, cr_ref, ci_ref):
    chunk = u_ref.shape[1]

    @pl.when(pl.program_id(1) == 0)
    def _():
        cr_ref[...] = jnp.zeros_like(cr_ref)
        ci_ref[...] = jnp.zeros_like(ci_ref)

    def start_states(end_r, end_i):
        last = chunk - SUBLANES
        lsr = pwr_ref[last:last + 1, :]
        lsi = pwi_ref[last:last + 1, :]
        sr = cr_ref[...]
        si = ci_ref[...]
        for s in range(SUBLANES):
            tr_ref[s:s + 1, :] = sr
            ti_ref[s:s + 1, :] = si
            sr, si = _cmul_add(lsr, lsi, sr, si, end_r[s:s + 1, :], end_i[s:s + 1, :])
        cr_ref[...] = sr
        ci_ref[...] = si

    _s5_body(u_ref, lbr_ref, lbi_ref, pwr_ref, pwi_ref, wbr_ref, wbi_ref, wcr_ref, wci_ref,
             dsk_ref, wg_ref, o_ref, hr_ref, hi_ref, tr_ref, ti_ref, start_states)
    hlr_ref[0] = cr_ref[...]
    hli_ref[0] = ci_ref[...]


def _s5_batch_kernel(u_ref, h0r_ref, h0i_ref, lbr_ref, lbi_ref, pwr_ref, pwi_ref, wbr_ref, wbi_ref,
                     wcr_ref, wci_ref, dsk_ref, wg_ref, o_ref, hlr_ref, hli_ref,
                     hr_ref, hi_ref, tr_ref, ti_ref):
    chunk = u_ref.shape[1]

    def start_states(end_r, end_i):
        last = chunk - SUBLANES
        tr = h0r_ref[0]
        ti = h0i_ref[0]
        tr_ref[...] = tr
        ti_ref[...] = ti
        fr, fi = _cmul_add(pwr_ref[last:last + SUBLANES, :], pwi_ref[last:last + SUBLANES, :],
                           tr, ti, end_r, end_i)
        hlr_ref[0] = fr
        hli_ref[0] = fi

    _s5_body(u_ref, lbr_ref, lbi_ref, pwr_ref, pwi_ref, wbr_ref, wbi_ref, wcr_ref, wci_ref,
             dsk_ref, wg_ref, o_ref, hr_ref, hi_ref, tr_ref, ti_ref, start_states)


def _to_segment_rows(u, chunk):
    *lead, s, d = u.shape
    seg = chunk // SUBLANES
    x = u.reshape(*lead, s // chunk, SUBLANES, seg, d)
    return jnp.swapaxes(x, -2, -3).reshape(*lead, s, d)


def _from_segment_rows(u, chunk):
    *lead, s, d = u.shape
    seg = chunk // SUBLANES
    x = u.reshape(*lead, s // chunk, seg, SUBLANES, d)
    return jnp.swapaxes(x, -2, -3).reshape(*lead, s, d)


def _s5_scratch(chunk):
    return [pltpu.VMEM((chunk, N_STATE), F32), pltpu.VMEM((chunk, N_STATE), F32),
            pltpu.VMEM((SUBLANES, N_STATE), F32), pltpu.VMEM((SUBLANES, N_STATE), F32)]


def _s5_seq(u, prep, weights, chunk):
    n, s, d = u.shape
    lbr, lbi, _, _, pwr, pwi = prep
    consts = (lbr, lbi, pwr, pwi) + tuple(weights)
    state = jax.ShapeDtypeStruct((n, 1, N_STATE), F32)
    return pl.pallas_call(
        _s5_seq_kernel, name="s5_seq",
        grid=(n, s // chunk),
        in_specs=[pl.BlockSpec((1, chunk, d), lambda b, i: (b, i, 0))]
                 + [_const_spec(a.shape) for a in consts],
        out_specs=(pl.BlockSpec((1, chunk, d), lambda b, i: (b, i, 0)),
                   pl.BlockSpec((1, 1, N_STATE), lambda b, i: (b, 0, 0)),
                   pl.BlockSpec((1, 1, N_STATE), lambda b, i: (b, 0, 0))),
        out_shape=(jax.ShapeDtypeStruct((n, s, d), F32), state, state),
        scratch_shapes=_s5_scratch(chunk) + [pltpu.VMEM((1, N_STATE), F32), pltpu.VMEM((1, N_STATE), F32)],
        compiler_params=_cparams(("arbitrary", "arbitrary")),
    )(u, *consts)


def _s5_batch(u, h0r, h0i, prep, weights):
    nblk, chunk, d = u.shape
    lbr, lbi, _, _, pwr, pwi = prep
    consts = (lbr, lbi, pwr, pwi) + tuple(weights)
    state = jax.ShapeDtypeStruct((nblk, SUBLANES, N_STATE), F32)
    blk = lambda shape: pl.BlockSpec(shape, lambda i: (i, 0, 0))
    return pl.pallas_call(
        _s5_batch_kernel, name="s5_batch",
        grid=(nblk,),
        in_specs=[blk((1, chunk, d)), blk((1, SUBLANES, N_STATE)), blk((1, SUBLANES, N_STATE))]
                 + [_const_spec(a.shape) for a in consts],
        out_specs=(blk((1, chunk, d)), blk((1, SUBLANES, N_STATE)), blk((1, SUBLANES, N_STATE))),
        out_shape=(jax.ShapeDtypeStruct((nblk, chunk, d), F32), state, state),
        scratch_shapes=_s5_scratch(chunk),
        compiler_params=_cparams(("arbitrary",)),
    )(u, h0r, h0i, *consts)


_PEER_CELLS = [(i, j) for i in range(PEER_TOPK) for j in range(PEER_TOPK)
               if (i + 1) * (j + 1) <= PEER_TOPK]
_N_CELLS_PAD = -(-len(_PEER_CELLS) // SUBLANES) * SUBLANES
_BIG_ID = 1 << 20


def _peer_route_kernel(xn_ref, wq_ref, keys_ref, cell_ref, a_ref, b_ref, w_ref,
                       q_ref, sv_ref, si_ref, cand_ref, cv_ref, fid_ref, ao_ref, bo_ref, wo_ref):
    t = xn_ref.shape[0]
    q_ref[...] = jnp.dot(xn_ref[...], wq_ref[...], preferred_element_type=F32)
    key_row = lax.broadcasted_iota(jnp.int32, (N_KEYS, t), 0)

    def sub_topk(ht, carry):
        qh = q_ref[:, pl.ds(pl.multiple_of(ht * D_HALF, D_HALF), D_HALF)].astype(BF16)
        s = lax.dot_general(keys_ref[ht], qh, (((1,), (1,)), ((), ())),
                            preferred_element_type=F32)
        for r in range(PEER_TOPK):
            m = jnp.max(s, axis=0, keepdims=True)
            idx = jnp.min(jnp.where(s == m, key_row, N_KEYS), axis=0, keepdims=True)
            s = jnp.where(key_row == idx, NEG_INF, s)
            sv_ref[ht, r:r + 1, :] = m
            si_ref[ht, r:r + 1, :] = idx
        return carry

    lax.fori_loop(0, 2 * PEER_HEADS, sub_topk, 0)

    cell_row = lax.broadcasted_iota(jnp.int32, (_N_CELLS_PAD, t), 0)
    cell_id = cell_ref[...]

    def combine(h, carry):
        va = sv_ref[2 * h]
        vb = sv_ref[2 * h + 1]
        cand_ref[...] = jnp.full((_N_CELLS_PAD, t), NEG_INF, F32)
        for c, (i, j) in enumerate(_PEER_CELLS):
            cand_ref[c:c + 1, :] = va[i:i + 1, :] + vb[j:j + 1, :]
        cand = cand_ref[...]
        for r in range(PEER_TOPK):
            m = jnp.max(cand, axis=0, keepdims=True)
            ridx = jnp.min(jnp.where(cand == m, cell_row, _N_CELLS_PAD), axis=0, keepdims=True)
            hit = cell_row == ridx
            cand = jnp.where(hit, NEG_INF, cand)
            cv_ref[r:r + 1, :] = m
            fid_ref[r:r + 1, :] = jnp.max(jnp.where(hit, cell_id, 0), axis=0, keepdims=True)
        cv = cv_ref[...]
        fid = fid_ref[...]
        isel = lax.shift_right_logical(fid, 4)
        jsel = jnp.bitwise_and(fid, PEER_TOPK - 1)
        ia = si_ref[2 * h]
        ib = si_ref[2 * h + 1]
        ka = jnp.zeros((PEER_TOPK, t), jnp.int32)
        kb = jnp.zeros((PEER_TOPK, t), jnp.int32)
        for r in range(PEER_TOPK):
            ka = jnp.where(isel == r, ia[r:r + 1, :], ka)
            kb = jnp.where(jsel == r, ib[r:r + 1, :], kb)
        e = jnp.exp(cv - cv[0:1, :])
        ao_ref[h] = ka
        bo_ref[h] = kb
        wo_ref[h] = e / jnp.sum(e, axis=0, keepdims=True)
        return carry

    lax.fori_loop(0, PEER_HEADS, combine, 0)
    nk = PEER_HEADS * PEER_TOPK
    a_ref[...] = ao_ref[...].reshape(nk, t).T
    b_ref[...] = bo_ref[...].reshape(nk, t).T
    w_ref[...] = wo_ref[...].reshape(nk, t).T


def _peer_route(xn, wq, keys, t):
    n, d = xn.shape
    nk = PEER_HEADS * PEER_TOPK
    ids = [i * PEER_TOPK + j for i, j in _PEER_CELLS] + [_BIG_ID] * (_N_CELLS_PAD - len(_PEER_CELLS))
    cell = jnp.broadcast_to(jnp.asarray(ids, jnp.int32)[:, None], (_N_CELLS_PAD, t))
    tok = pl.BlockSpec((t, nk), lambda i: (i, 0))
    return pl.pallas_call(
        _peer_route_kernel, name="peer_route",
        grid=(n // t,),
        in_specs=[pl.BlockSpec((t, d), lambda i: (i, 0)), _const_spec(wq.shape),
                  _const_spec(keys.shape), _const_spec(cell.shape)],
        out_specs=(tok, tok, tok),
        out_shape=(jax.ShapeDtypeStruct((n, nk), jnp.int32), jax.ShapeDtypeStruct((n, nk), jnp.int32),
                   jax.ShapeDtypeStruct((n, nk), F32)),
        scratch_shapes=[pltpu.VMEM((t, wq.shape[1]), F32),
                        pltpu.VMEM((2 * PEER_HEADS, PEER_TOPK, t), F32),
                        pltpu.VMEM((2 * PEER_HEADS, PEER_TOPK, t), jnp.int32),
                        pltpu.VMEM((_N_CELLS_PAD, t), F32),
                        pltpu.VMEM((PEER_TOPK, t), F32), pltpu.VMEM((PEER_TOPK, t), jnp.int32),
                        pltpu.VMEM((PEER_HEADS, PEER_TOPK, t), jnp.int32),
                        pltpu.VMEM((PEER_HEADS, PEER_TOPK, t), jnp.int32),
                        pltpu.VMEM((PEER_HEADS, PEER_TOPK, t), F32)],
        compiler_params=_cparams(("arbitrary",)),
    )(xn, wq, keys, cell)


EXPERT_BLOCK = 2048
COEF_PITCH = 72
HALF_GROUPS = N_KEYS // 2


def _peer_hidden_kernel(xn_ref, a_ref, b_ref, w_ref, u_ref, c_ref, hsel_ref):
    j = pl.program_id(1)
    groups = u_ref.shape[0] // N_KEYS

    @pl.when(j == 0)
    def _():
        hsel_ref[...] = jnp.zeros_like(hsel_ref)

    h = lax.dot_general(xn_ref[...], u_ref[...], (((1,), (1,)), ((), ())),
                        preferred_element_type=F32)
    a_idx = a_ref[...]
    b_idx = b_ref[...]
    hs = hsel_ref[...]
    for g in range(groups):
        hg = h[:, g * N_KEYS:(g + 1) * N_KEYS]
        hs = jnp.where(a_idx == j * groups + g, jnp.take_along_axis(hg, b_idx, axis=1), hs)
    hsel_ref[...] = hs

    @pl.when(j == pl.num_programs(1) - 1)
    def _():
        c_ref[...] = w_ref[...] * jax.nn.gelu(hs, approximate=True)


def _peer_hidden(xn, a_idx, b_idx, w, u_bf16, t):
    n, d = xn.shape
    nk = a_idx.shape[1]
    ne = u_bf16.shape[0]
    tok = lambda width: pl.BlockSpec((t, width), lambda i, j: (i, 0))
    return pl.pallas_call(
        _peer_hidden_kernel, name="peer_hidden",
        grid=(n // t, ne // EXPERT_BLOCK),
        in_specs=[tok(d), tok(nk), tok(nk), tok(nk),
                  pl.BlockSpec((EXPERT_BLOCK, d), lambda i, j: (j, 0))],
        out_specs=tok(nk),
        out_shape=jax.ShapeDtypeStruct((n, nk), F32),
        scratch_shapes=[pltpu.VMEM((t, nk), F32)],
        compiler_params=_cparams(("arbitrary", "arbitrary")),
    )(xn, a_idx, b_idx, w, u_bf16)


def _peer_out_kernel(a_ref, b_ref, c_ref, x1_ref, v_ref, y_ref, s_ref, acc_ref):
    j = pl.program_id(1)
    t = a_ref.shape[0]
    pairs = v_ref.shape[0] // (2 * N_KEYS)
    hi_mask = jnp.uint32(0xFFFF0000)

    @pl.when(j == 0)
    def _():
        sub = lax.broadcasted_iota(jnp.int32, (N_KEYS, N_KEYS), 0)

        def one_token(i, carry):
            arow = a_ref[pl.ds(i, 1), :]
            brow = b_ref[pl.ds(i, 1), :]
            crow = c_ref[pl.ds(i, 1), :]
            pt = jnp.where(sub == arow, 1.0, 0.0).astype(BF16)
            rt = jnp.where(sub == brow, crow, 0.0).astype(BF16)
            coef = lax.dot_general(pt, rt, (((1,), (1,)), ((), ())), preferred_element_type=F32)
            bits = pltpu.bitcast(coef.astype(BF16).astype(F32), jnp.uint32)
            packed = jnp.bitwise_or(jnp.bitwise_and(bits[:HALF_GROUPS], hi_mask),
                                    lax.shift_right_logical(bits[HALF_GROUPS:], jnp.uint32(16)))
            s_ref[pl.ds(pl.multiple_of(i * COEF_PITCH, SUBLANES), HALF_GROUPS), :] = packed
            return carry

        lax.fori_loop(0, t, one_token, 0, unroll=32)

    parts = []
    for g in range(pairs):
        w32 = s_ref[pl.ds(j * pairs + g, t, stride=COEF_PITCH), :]
        parts.append(pltpu.bitcast(jnp.bitwise_and(w32, hi_mask), F32).astype(BF16))
        parts.append(pltpu.bitcast(lax.shift_left(w32, jnp.uint32(16)), F32).astype(BF16))
    coef_blk = jnp.concatenate(parts, axis=1)
    contrib = jnp.dot(coef_blk, v_ref[...], preferred_element_type=F32)

    @pl.when(j == 0)
    def _():
        acc_ref[...] = contrib

    @pl.when(j > 0)
    def _():
        acc_ref[...] += contrib

    @pl.when(j == pl.num_programs(1) - 1)
    def _():
        y_ref[...] = x1_ref[...] + acc_ref[...]


def _pair_groups(table_bf16):
    ne, d = table_bf16.shape
    x = table_bf16.reshape(2, HALF_GROUPS, N_KEYS, d)
    return jnp.swapaxes(x, 0, 1).reshape(ne, d)


def _peer_out(a_idx, b_idx, c, x1, v_paired, t):
    n, d = x1.shape
    nk = a_idx.shape[1]
    ne = v_paired.shape[0]
    tok = lambda width: pl.BlockSpec((t, width), lambda i, j: (i, 0))
    return pl.pallas_call(
        _peer_out_kernel, name="peer_out",
        grid=(n // t, ne // EXPERT_BLOCK),
        in_specs=[tok(nk), tok(nk), tok(nk), tok(d),
                  pl.BlockSpec((EXPERT_BLOCK, d), lambda i, j: (j, 0))],
        out_specs=tok(d),
        out_shape=jax.ShapeDtypeStruct((n, d), F32),
        scratch_shapes=[pltpu.VMEM((t * COEF_PITCH, N_KEYS), jnp.uint32), pltpu.VMEM((t, d), F32)],
        compiler_params=_cparams(("arbitrary", "arbitrary")),
    )(a_idx, b_idx, c, x1, v_paired)


TOKEN_TILE = 512
ATTN_TILE = 1024
S5_CHUNK = 256
ROUTE_TILE = 512
EXPERT_TILE = 512


def _tile(n, pref):
    t = min(pref, n)
    while n % t:
        t //= 2
    return t


def _tail(x1, xn2, lw):
    n = x1.shape[0]
    a_idx, b_idx, w = _peer_route(xn2, lw['wq'], lw['keys'], _tile(n, ROUTE_TILE))
    te = _tile(n, EXPERT_TILE)
    c = _peer_hidden(xn2, a_idx, b_idx, w, lw['u'], te)
    return _peer_out(a_idx, b_idx, c, x1, lw['v'], te)


def _layer_weights(l, norm1_g, w_in, b_forget, q_norm_g, k_norm_g, ssm_lam_re, ssm_lam_im, ssm_log_dt,
                   ssm_b_re, ssm_b_im, ssm_c_re, ssm_c_im, ssm_d, w_glu, w_proj_attn, w_proj_ssm, w_out,
                   norm2_g, w_query, sub_keys, expert_u, expert_v, chunks):
    lw = {'inproj': _prep_inproj(norm1_g[l], w_in[l], b_forget[l], q_norm_g[l], k_norm_g[l])}
    lw['s5_prep'] = {c: _s5_prep(ssm_lam_re[l], ssm_lam_im[l], ssm_log_dt[l], ssm_b_re[l], ssm_b_im[l], c)
                     for c in chunks}
    p0 = lw['s5_prep'][chunks[0]]
    lw['s5_w'] = _s5_weights(p0[2], p0[3], ssm_c_re[l], ssm_c_im[l], ssm_d[l], w_glu[l])
    lw['mix'] = (w_proj_attn[l].astype(BF16), w_proj_ssm[l].astype(BF16), w_out[l].astype(BF16),
                 norm2_g[l].astype(F32)[None, :])
    lw['wq'] = w_query[l].astype(BF16)
    lw['keys'] = sub_keys[l].reshape(2 * PEER_HEADS, N_KEYS, D_HALF).astype(BF16)
    lw['u'] = expert_u[l].astype(BF16)
    lw['v'] = _pair_groups(expert_v[l].astype(BF16))
    return lw


def kernel(x_prompt, x_sample, cache_k, cache_v, cache_logf, state_ssm_re, state_ssm_im, page_table,
           norm1_g, w_in, b_forget, q_norm_g, k_norm_g,
           ssm_lam_re, ssm_lam_im, ssm_log_dt, ssm_b_re, ssm_b_im, ssm_c_re, ssm_c_im, ssm_d, w_glu,
           w_proj_attn, w_proj_ssm, w_out, norm2_g, w_query, sub_keys, expert_u, expert_v):
    nb, s, d = x_prompt.shape
    db, t, _ = x_sample.shape
    depth = w_in.shape[0]
    assert db % SUBLANES == 0 and t * N_HEADS <= LANES and t <= PAGE
    batch_chunk = SUBLANES * t
    y_p = x_prompt.reshape(nb * s, d)
    y_s = x_sample.reshape(db * t, d)
    outs = [[] for _ in range(10)]
    for l in range(depth):
        lw = _layer_weights(l, norm1_g, w_in, b_forget, q_norm_g, k_norm_g, ssm_lam_re, ssm_lam_im,
                            ssm_log_dt, ssm_b_re, ssm_b_im, ssm_c_re, ssm_c_im, ssm_d, w_glu,
                            w_proj_attn, w_proj_ssm, w_out, norm2_g, w_query, sub_keys, expert_u,
                            expert_v, (S5_CHUNK, batch_chunk))

        tm = _tile(nb * s, TOKEN_TILE)
        qb, kb, vb, kf, vf, lf, u, ga, gs = _inproj(y_p, *lw['inproj'], tm=tm)
        lf3 = lf.reshape(nb, s, N_HEADS)
        ta = _tile(s, ATTN_TILE)
        ct = _cumsum(jnp.swapaxes(lf3, 1, 2), ta)
        o_attn = _attn_prompt(qb.reshape(nb, s, D_ATTN), kb.reshape(nb, s, D_ATTN),
                              vb.reshape(nb, s, D_ATTN), ct, ta)
        u_seg = _to_segment_rows(u.reshape(nb, s, D_SSM), S5_CHUNK)
        o_ssm, hr, hi = _s5_seq(u_seg, lw['s5_prep'][S5_CHUNK], lw['s5_w'], S5_CHUNK)
        o_ssm = _from_segment_rows(o_ssm, S5_CHUNK).reshape(nb * s, D_SSM)
        x1, xn2 = _mix(o_attn.reshape(nb * s, D_ATTN), o_ssm, ga, gs, y_p, *lw['mix'], tm=tm)
        y_p = _tail(x1, xn2, lw)
        for slot, val in zip(range(5), (kf.reshape(nb, s, N_HEADS, HEAD_DIM),
                                        vf.reshape(nb, s, N_HEADS, HEAD_DIM), lf3,
                                        hr.reshape(nb, N_GROUPS, STATE_DIM),
                                        hi.reshape(nb, N_GROUPS, STATE_DIM))):
            outs[slot].append(val)

        tm = _tile(db * t, TOKEN_TILE)
        qb, kb, vb, kf, vf, lf, u, ga, gs = _inproj(y_s, *lw['inproj'], tm=tm)
        o_attn = _attn_paged(page_table, qb, kf, vf, lf, cache_k, cache_v, cache_logf, t, l).astype(BF16)
        nblk = db // SUBLANES
        u_seg = _to_segment_rows(u.reshape(nblk, batch_chunk, D_SSM), batch_chunk)
        h0r = state_ssm_re[l].astype(F32).reshape(nblk, SUBLANES, N_STATE)
        h0i = state_ssm_im[l].astype(F32).reshape(nblk, SUBLANES, N_STATE)
        o_ssm, hr, hi = _s5_batch(u_seg, h0r, h0i, lw['s5_prep'][batch_chunk], lw['s5_w'])
        o_ssm = _from_segment_rows(o_ssm, batch_chunk).reshape(db * t, D_SSM)
        x1, xn2 = _mix(o_attn, o_ssm, ga, gs, y_s, *lw['mix'], tm=tm)
        y_s = _tail(x1, xn2, lw)
        for slot, val in zip(range(5, 10), (kf.reshape(db, t, N_HEADS, HEAD_DIM),
                                            vf.reshape(db, t, N_HEADS, HEAD_DIM),
                                            lf.reshape(db, t, N_HEADS),
                                            hr.reshape(db, N_GROUPS, STATE_DIM),
                                            hi.reshape(db, N_GROUPS, STATE_DIM))):
            outs[slot].append(val)
    stacked = [jnp.stack(o) for o in outs]
    return (y_p.reshape(nb, s, d), y_s.reshape(db, t, d), *stacked)
```

```python
import functools
import math

import jax
import jax.numpy as jnp
from jax import lax
from jax.experimental import pallas as pl
from jax.experimental.pallas import tpu as pltpu

F32 = jnp.float32
BF16 = jnp.bfloat16

N_HEADS = 8
HEAD_DIM = 64
D_ATTN = N_HEADS * HEAD_DIM
ATTN_SCALE = HEAD_DIM ** -0.5
LOG2E = math.log2(math.e)
D_SSM = 512
SSM_GROUP = 16
N_GROUPS = D_SSM // SSM_GROUP
STATE_DIM = 64
N_STATE = N_GROUPS * STATE_DIM
PEER_HEADS = 8
N_KEYS = 128
PEER_TOPK = 16
D_HALF = 128
NORM_EPS = 1e-6
NEG_INF = -1e30
PAGE = 128

LANES = 128
SUBLANES = 8
VMEM_LIMIT = 56 * 1024 * 1024


def _cparams(sem):
    return pltpu.CompilerParams(dimension_semantics=sem, vmem_limit_bytes=VMEM_LIMIT)


def _const_spec(shape):
    nd = len(shape)
    return pl.BlockSpec(shape, lambda *_: (0,) * nd)


def _rms_rows(x, g):
    r = lax.rsqrt(jnp.mean(x * x, axis=-1, keepdims=True) + NORM_EPS)
    return (x * r) * g


def _split_dot(a, b_bf16):
    hi = a.astype(BF16)
    lo = (a - hi.astype(F32)).astype(BF16)
    return (jnp.dot(hi, b_bf16, preferred_element_type=F32)
            + jnp.dot(lo, b_bf16, preferred_element_type=F32))


def _log_sigmoid(x):
    return jnp.minimum(x, 0.0) - jnp.log1p(jnp.exp(-jnp.abs(x)))


def _inproj_kernel(x_ref, g1_ref, w_ref, wf_ref, bf_ref, gq_ref, gk_ref, hsel_ref,
                   qb_ref, kb_ref, vb_ref, kf_ref, vf_ref, lf_ref, u_ref, ga_ref, gs_ref, *, key_minor):
    xn = _rms_rows(x_ref[...], g1_ref[...]).astype(BF16)

    def seg(i, n):
        return jnp.dot(xn, w_ref[:, i:i + n], preferred_element_type=F32)

    hsel = hsel_ref[...]

    def head_norm(z, g):
        ms = _split_dot(z * z, hsel)
        return (z * lax.rsqrt(ms + NORM_EPS)) * g

    q = head_norm(seg(0, D_ATTN), gq_ref[...])
    qb_ref[...] = (q * (ATTN_SCALE * LOG2E)).astype(BF16)
    k = head_norm(seg(D_ATTN, D_ATTN), gk_ref[...])
    kb_ref[...] = k.astype(BF16)
    v = seg(2 * D_ATTN, D_ATTN)
    vb_ref[...] = v.astype(BF16)
    if key_minor:
        kf_ref[0] = k.T
        vf_ref[0] = v.T
    else:
        kf_ref[...] = k
        vf_ref[...] = v
    u_ref[...] = seg(3 * D_ATTN, D_SSM)
    o = 3 * D_ATTN + D_SSM
    d = ga_ref.shape[1]
    ga_ref[...] = jax.nn.sigmoid(seg(o, d)).astype(BF16)
    gs_ref[...] = jax.nn.sigmoid(seg(o + d, d)).astype(BF16)
    zf = jnp.dot(xn, wf_ref[...], preferred_element_type=F32) + bf_ref[...]
    lf_ref[...] = _log_sigmoid(zf)[:, :N_HEADS]


def _prep_inproj(norm1_g, w_in, b_forget, q_norm_g, k_norm_g):
    o3 = 3 * D_ATTN
    o4 = o3 + N_HEADS
    wmain = jnp.concatenate([w_in[:, :o3], w_in[:, o4:]], axis=1).astype(BF16)
    wf = jnp.pad(w_in[:, o3:o4], ((0, 0), (0, LANES - N_HEADS))).astype(BF16)
    bfp = jnp.pad(b_forget.astype(F32), (0, LANES - N_HEADS))[None, :]
    gq = jnp.tile(q_norm_g.astype(F32), N_HEADS)[None, :]
    gk = jnp.tile(k_norm_g.astype(F32), N_HEADS)[None, :]
    hsel = jnp.kron(jnp.eye(N_HEADS, dtype=F32),
                    jnp.full((HEAD_DIM, HEAD_DIM), 1.0 / HEAD_DIM, F32)).astype(BF16)
    return norm1_g.astype(F32)[None, :], wmain, wf, bfp, gq, gk, hsel


def _inproj(x, g1, wmain, wf, bfp, gq, gk, hsel, tm, seq_len=None):
    n, d = x.shape
    tok = lambda w: pl.BlockSpec((tm, w), lambda i: (i, 0))
    if seq_len is None:
        kv_shape = jax.ShapeDtypeStruct((n, D_ATTN), F32)
        kv_spec = tok(D_ATTN)
    else:
        per_seq = seq_len // tm
        kv_shape = jax.ShapeDtypeStruct((n // seq_len, D_ATTN, seq_len), F32)
        kv_spec = pl.BlockSpec((1, D_ATTN, tm), lambda i: (i // per_seq, 0, i % per_seq))
    out_shape = (
        jax.ShapeDtypeStruct((n, D_ATTN), BF16), jax.ShapeDtypeStruct((n, D_ATTN), BF16),
        jax.ShapeDtypeStruct((n, D_ATTN), BF16), kv_shape, kv_shape,
        jax.ShapeDtypeStruct((n, N_HEADS), F32),
        jax.ShapeDtypeStruct((n, D_SSM), F32), jax.ShapeDtypeStruct((n, d), BF16),
        jax.ShapeDtypeStruct((n, d), BF16))
    return pl.pallas_call(
        functools.partial(_inproj_kernel, key_minor=seq_len is not None), name="inproj",
        grid=(n // tm,),
        in_specs=[tok(d), _const_spec(g1.shape), _const_spec(wmain.shape), _const_spec(wf.shape),
                  _const_spec(bfp.shape), _const_spec(gq.shape), _const_spec(gk.shape),
                  _const_spec(hsel.shape)],
        out_specs=(tok(D_ATTN), tok(D_ATTN), tok(D_ATTN), kv_spec, kv_spec, tok(N_HEADS),
                   tok(D_SSM), tok(d), tok(d)),
        out_shape=out_shape,
        compiler_params=_cparams(("arbitrary",)),
    )(x, g1, wmain, wf, bfp, gq, gk, hsel)


def _mix_kernel(oa_ref, os_ref, ga_ref, gs_ref, x_ref, wpa_ref, wps_ref, wo_ref, g2_ref,
                x1_ref, xn_ref):
    pa = jnp.dot(oa_ref[...], wpa_ref[...], preferred_element_type=F32)
    ps = jnp.dot(os_ref[...].astype(BF16), wps_ref[...], preferred_element_type=F32)
    mixed = ga_ref[...].astype(F32) * pa + gs_ref[...].astype(F32) * ps
    x1 = x_ref[...] + jnp.dot(mixed.astype(BF16), wo_ref[...], preferred_element_type=F32)
    x1_ref[...] = x1
    xn_ref[...] = _rms_rows(x1, g2_ref[...]).astype(BF16)


def _mix(oa, os_, ga, gs, x, wpa, wps, wo, g2, tm):
    n, d = x.shape
    tok = lambda w: pl.BlockSpec((tm, w), lambda i: (i, 0))
    return pl.pallas_call(
        _mix_kernel, name="mix",
        grid=(n // tm,),
        in_specs=[tok(D_ATTN), tok(D_SSM), tok(d), tok(d), tok(d), _const_spec(wpa.shape),
                  _const_spec(wps.shape), _const_spec(wo.shape), _const_spec(g2.shape)],
        out_specs=(tok(d), tok(d)),
        out_shape=(jax.ShapeDtypeStruct((n, d), F32), jax.ShapeDtypeStruct((n, d), BF16)),
        compiler_params=_cparams(("arbitrary",)),
    )(oa, os_, ga, gs, x, wpa, wps, wo, g2)


def _split3(a):
    parts = []
    rem = a
    for _ in range(3):
        part = rem.astype(BF16)
        rem = rem - part.astype(F32)
        parts.append(part)
    return parts


def _dot01_right(a, b01):
    return sum(jnp.dot(p, b01, preferred_element_type=F32) for p in _split3(a))


def _cumsum_kernel(lft_ref, ct_ref, carry_ref):
    t = lft_ref.shape[2]

    @pl.when(pl.program_id(1) == 0)
    def _():
        carry_ref[...] = jnp.zeros_like(carry_ref)

    r = lax.broadcasted_iota(jnp.int32, (t, t), 0)
    c = lax.broadcasted_iota(jnp.int32, (t, t), 1)
    upto = jnp.where(r <= c, 1.0, 0.0).astype(BF16)
    cs = carry_ref[...] + _dot01_right(lft_ref[0], upto)
    ct_ref[0] = cs * LOG2E
    carry_ref[...] = cs[:, t - 1:t]


def _cumsum(lft, t):
    n, h, s = lft.shape
    return pl.pallas_call(
        _cumsum_kernel, name="logf_cumsum",
        grid=(n, s // t),
        in_specs=[pl.BlockSpec((1, h, t), lambda b, i: (b, 0, i))],
        out_specs=pl.BlockSpec((1, h, t), lambda b, i: (b, 0, i)),
        out_shape=jax.ShapeDtypeStruct((n, h, s), F32),
        scratch_shapes=[pltpu.VMEM((h, 1), F32)],
        compiler_params=_cparams(("arbitrary", "arbitrary")),
    )(lft)


ATTN_ROW_BLOCK = 256


def _attn_prompt_kernel(first_ref, q_ref, k_ref, v_ref, ct_ref, o_ref, m_ref, acc_ref,
                        s0_ref, s1_ref, p0_ref, p1_ref, a0_ref, a1_ref):
    tq = q_ref.shape[1]
    tk = k_ref.shape[1]
    rb_rows = min(ATTN_ROW_BLOCK, tq)
    n_rb = tq // rb_rows
    n_iter = (N_HEADS // 2) * n_rb
    qi = pl.program_id(1)
    ki = first_ref[pl.program_id(0), qi] + pl.program_id(2)
    s_refs = (s0_ref, s1_ref)
    p_refs = (p0_ref, p1_ref)
    a_refs = (a0_ref, a1_ref)

    @pl.when(pl.program_id(2) == 0)
    def _():
        m_ref[...] = jnp.full_like(m_ref, NEG_INF)
        acc_ref[...] = jnp.zeros_like(acc_ref)

    lane = lax.broadcasted_iota(jnp.int32, (tk, LANES), 1)
    keep = (jnp.where(lane < HEAD_DIM, 1.0, 0.0).astype(BF16),
            jnp.where(lane < HEAD_DIM, 0.0, 1.0).astype(BF16))
    ones = (jnp.where(lane == HEAD_DIM, 1.0, 0.0).astype(BF16),
            jnp.where(lane == 0, 1.0, 0.0).astype(BF16))

    def where_is(j):
        pair = j // n_rb
        return pair, pl.multiple_of((j - pair * n_rb) * rb_rows, rb_rows)

    def slab(pair):
        return pl.ds(pl.multiple_of(pair * LANES, LANES), LANES)

    def scores(j, half):
        pair, row0 = where_is(j)
        qblk = q_ref[0, pl.ds(row0, rb_rows), slab(pair)]
        kh = k_ref[0, :, slab(pair)] * keep[half]
        s_refs[half][...] = lax.dot_general(qblk, kh, (((1,), (1,)), ((), ())),
                                            preferred_element_type=F32)

    def softmax(j, half, diagonal):
        pair, row0 = where_is(j)
        h = 2 * pair + half
        ckb = jnp.broadcast_to(ct_ref[0, pl.ds(h, 1), :], (SUBLANES, tk))
        if diagonal:
            col = lax.broadcasted_iota(jnp.int32, (SUBLANES, tk), 1)
            sub = lax.broadcasted_iota(jnp.int32, (SUBLANES, tk), 0) + row0
        probs = None
        for g in range(rb_rows // SUBLANES):
            lo = g * SUBLANES
            s = s_refs[half][lo:lo + SUBLANES, :] - ckb
            if diagonal:
                s = jnp.where(col <= sub + lo, s, NEG_INF)
            rows = pl.ds(row0 + lo, SUBLANES)
            m_prev = m_ref[h, rows, :]
            m_new = jnp.maximum(m_prev, jnp.max(s, axis=-1, keepdims=True))
            a_refs[half][lo:lo + SUBLANES, :] = jnp.exp2(m_prev - m_new)
            m_ref[h, rows, :] = m_new
            prob = jnp.exp2(s - m_new)
            if g % 2 == 0:
                probs = prob
            else:
                p_refs[half][lo - SUBLANES:lo + SUBLANES, :] = (
                    jnp.concatenate([probs, prob], axis=0).astype(BF16))

    def values(j, half):
        pair, row0 = where_is(j)
        h = 2 * pair + half
        vh = v_ref[0, :, slab(pair)] * keep[half] + ones[half]
        pv = jnp.dot(p_refs[half][...], vh, preferred_element_type=F32)
        rows = pl.ds(row0, rb_rows)
        acc_ref[h, rows, :] = a_refs[half][...] * acc_ref[h, rows, :] + pv

    def step(diagonal):
        scores(0, 0)
        scores(0, 1)
        softmax(0, 0, diagonal)

        def body(j, carry):
            scores(j, 0)
            softmax(j - 1, 1, diagonal)
            values(j - 1, 0)
            scores(j, 1)
            softmax(j, 0, diagonal)
            values(j - 1, 1)
            return carry

        lax.fori_loop(1, n_iter, body, 0)
        softmax(n_iter - 1, 1, diagonal)
        values(n_iter - 1, 0)
        values(n_iter - 1, 1)

    @pl.when(ki < qi)
    def _():
        step(False)

    @pl.when(ki == qi)
    def _():
        step(True)
        lane_q = lax.broadcasted_iota(jnp.int32, (tq, LANES), 1) < HEAD_DIM
        for hp in range(N_HEADS // 2):
            lo = acc_ref[2 * hp]
            hi = acc_ref[2 * hp + 1]
            out = jnp.where(lane_q, lo / lo[:, HEAD_DIM:HEAD_DIM + 1], hi / hi[:, 0:1])
            o_ref[0, :, hp * LANES:(hp + 1) * LANES] = out.astype(o_ref.dtype)


UNDERFLOW_LOG2 = 160.0


def _first_key_block(qb, kb, ct, t):
    n, s, d = qb.shape
    nb = s // t

    def block_norm(x):
        x2 = jnp.square(x.astype(F32)).reshape(n, nb, t, N_HEADS, HEAD_DIM).sum(-1)
        return jnp.sqrt(x2.max(axis=2)) * 1.02
    qn = block_norm(qb)
    kn = block_norm(kb)
    c_start = jnp.swapaxes(ct[:, :, 0::t], 1, 2)
    c_end = jnp.swapaxes(ct[:, :, t - 1::t], 1, 2)
    bound = (qn[:, :, None] * kn[:, None, :] + (qn * kn)[:, :, None]
             + c_start[:, :, None] - c_end[:, None, :])
    needed = jnp.any(bound >= -UNDERFLOW_LOG2, axis=-1)
    i_idx = jnp.arange(nb)[:, None]
    j_idx = jnp.arange(nb)[None, :]
    needed = jnp.where(j_idx < i_idx, needed, j_idx == i_idx)
    return jnp.argmax(needed, axis=-1).astype(jnp.int32)


def _attn_prompt(qb, kb, vb, ct, t):
    n, s, d = qb.shape
    nb = s // t
    rb = min(ATTN_ROW_BLOCK, t)
    first = _first_key_block(qb, kb, ct, t)
    qmap = lambda b, i, j, first: (b, i, 0)
    kblock = lambda b, i, j, first: jnp.minimum(first[b, i] + j, i)
    kmap = lambda b, i, j, first: (b, kblock(b, i, j, first), 0)
    grid_spec = pltpu.PrefetchScalarGridSpec(
        num_scalar_prefetch=1,
        grid=(n, nb, nb),
        in_specs=[pl.BlockSpec((1, t, d), qmap), pl.BlockSpec((1, t, d), kmap),
                  pl.BlockSpec((1, t, d), kmap),
                  pl.BlockSpec((1, N_HEADS, t), lambda b, i, j, first: (b, 0, kblock(b, i, j, first)))],
        out_specs=pl.BlockSpec((1, t, d), qmap),
        scratch_shapes=[pltpu.VMEM((N_HEADS, t, 1), F32), pltpu.VMEM((N_HEADS, t, LANES), F32),
                        pltpu.VMEM((rb, t), F32), pltpu.VMEM((rb, t), F32),
                        pltpu.VMEM((rb, t), BF16), pltpu.VMEM((rb, t), BF16),
                        pltpu.VMEM((rb, 1), F32), pltpu.VMEM((rb, 1), F32)])
    return pl.pallas_call(
        _attn_prompt_kernel, name="attn_prompt",
        grid_spec=grid_spec,
        out_shape=jax.ShapeDtypeStruct((n, s, d), BF16),
        compiler_params=_cparams(("arbitrary", "arbitrary", "arbitrary")),
    )(first, qb, kb, vb, ct)


PAGES_PER_STEP = 16


def _attn_paged_kernel(*refs, n_new, group):
    pt_ref, q_ref, kn_ref, vn_ref, lfn_ref = refs[:5]
    kc_refs = refs[5:5 + group]
    vc_refs = refs[5 + group:5 + 2 * group]
    lfc_refs = refs[5 + 2 * group:5 + 3 * group]
    o_ref, m_ref, l_ref, acc_ref, carry_ref, cq_ref = refs[5 + 3 * group:]
    del pt_ref
    p = pl.program_id(1)
    rows = n_new * N_HEADS
    krow = lax.broadcasted_iota(jnp.int32, (PAGE, PAGE), 0)
    kcol = lax.broadcasted_iota(jnp.int32, (PAGE, PAGE), 1)
    later = jnp.where(krow > kcol, 1.0, 0.0).astype(BF16)
    qbd = q_ref[0]

    def suffix_in_page(lf_page):
        loc = _dot01_right(lf_page, later)
        return loc, loc[:, 0:1] + lf_page[:, 0:1]

    def visit(pages, carry, valid):
        scores, vals = [], []
        for k_page, v_page, lf_page in pages:
            loc, total = suffix_in_page(lf_page)
            bias = jnp.concatenate([(loc + carry) * LOG2E] * n_new, axis=0)
            s = jnp.dot(qbd, k_page.astype(BF16), preferred_element_type=F32) + bias
            scores.append(s)
            vals.append(v_page.astype(BF16))
            carry = carry + total
        s = jnp.concatenate(scores, axis=1)
        if valid is not None:
            s = jnp.where(valid, s, NEG_INF)
        cq = cq_ref[...]
        m_prev = m_ref[...]
        m_new = jnp.maximum(m_prev, jnp.max(s, axis=1, keepdims=True) + cq)
        prob = jnp.exp2(s - (m_new - cq))
        alpha = jnp.exp2(m_prev - m_new)
        l_ref[...] = alpha * l_ref[...] + jnp.sum(prob, axis=1, keepdims=True)
        m_ref[...] = m_new
        pb = prob.astype(BF16)
        pv = None
        for i, v in enumerate(vals):
            t = lax.dot_general(pb[:, i * PAGE:(i + 1) * PAGE], v, (((1,), (1,)), ((), ())),
                                preferred_element_type=F32)
            pv = t if pv is None else pv + t
        acc_ref[...] = alpha * acc_ref[...] + pv
        return carry

    @pl.when(p == 0)
    def _():
        row = lax.broadcasted_iota(jnp.int32, (rows, PAGE), 0)
        col = lax.broadcasted_iota(jnp.int32, (rows, PAGE), 1)
        qry = lax.shift_right_logical(row, 3)
        lf_new = lfn_ref[0]
        loc, _ = suffix_in_page(lf_new)
        tiled = jnp.concatenate([loc] * n_new, axis=0)
        cq_ref[...] = -LOG2E * jnp.sum(jnp.where(col == qry, tiled, 0.0), axis=1, keepdims=True)
        m_ref[...] = jnp.full_like(m_ref, NEG_INF)
        l_ref[...] = jnp.zeros_like(l_ref)
        acc_ref[...] = jnp.zeros_like(acc_ref)
        carry_ref[...] = visit([(kn_ref[0], vn_ref[0], lf_new)], jnp.zeros((N_HEADS, 1), F32),
                               col <= qry)

    @pl.when(p > 0)
    def _():
        pages = [(kc_refs[i][0, 0], vc_refs[i][0, 0], lfc_refs[i][0, 0]) for i in range(group)]
        carry_ref[...] = visit(pages, carry_ref[...], None)

    @pl.when(p == pl.num_programs(1) - 1)
    def _():
        o_ref[0] = acc_ref[...] / l_ref[...]


def _attn_paged(page_table, qb, kf, vf, lf, cache_k, cache_v, cache_logf, n_new, layer):
    nb, n_pages = page_table.shape
    rows = n_new * N_HEADS
    group = _tile(n_pages, PAGES_PER_STEP)
    ck = jnp.transpose(cache_k, (0, 1, 3, 4, 2)).reshape(cache_k.shape[0], -1, D_ATTN, PAGE)
    cv = jnp.transpose(cache_v, (0, 1, 3, 4, 2)).reshape(cache_v.shape[0], -1, D_ATTN, PAGE)
    cl = jnp.transpose(cache_logf, (0, 1, 3, 2))
    q4 = qb.reshape(nb, n_new, N_HEADS, HEAD_DIM)
    qbd = jnp.einsum('bthd,gh->btghd', q4, jnp.eye(N_HEADS, dtype=qb.dtype)).reshape(nb, rows, D_ATTN)
    key_minor = lambda x: jnp.pad(jnp.swapaxes(x.reshape(nb, n_new, -1), 1, 2),
                                  ((0, 0), (0, 0), (0, PAGE - n_new)))
    per_seq = lambda b, p, pt: (b, 0, 0)

    def page(i):
        return lambda b, p, pt: (layer, pt[b, n_pages - 1 - (jnp.maximum(p, 1) - 1) * group - i], 0, 0)

    in_specs = [pl.BlockSpec((1, rows, D_ATTN), per_seq), pl.BlockSpec((1, D_ATTN, PAGE), per_seq),
                pl.BlockSpec((1, D_ATTN, PAGE), per_seq), pl.BlockSpec((1, N_HEADS, PAGE), per_seq)]
    in_specs += [pl.BlockSpec((1, 1, D_ATTN, PAGE), page(i)) for i in range(group)] * 2
    in_specs += [pl.BlockSpec((1, 1, N_HEADS, PAGE), page(i)) for i in range(group)]
    grid_spec = pltpu.PrefetchScalarGridSpec(
        num_scalar_prefetch=1,
        grid=(nb, n_pages // group + 1),
        in_specs=in_specs,
        out_specs=pl.BlockSpec((1, rows, D_ATTN), per_seq),
        scratch_shapes=[pltpu.VMEM((rows, 1), F32), pltpu.VMEM((rows, 1), F32),
                        pltpu.VMEM((rows, D_ATTN), F32), pltpu.VMEM((N_HEADS, 1), F32),
                        pltpu.VMEM((rows, 1), F32)])
    o = pl.pallas_call(
        functools.partial(_attn_paged_kernel, n_new=n_new, group=group), name="attn_paged",
        grid_spec=grid_spec,
        out_shape=jax.ShapeDtypeStruct((nb, rows, D_ATTN), F32),
        compiler_params=_cparams(("arbitrary", "arbitrary")),
    )(page_table, qbd, key_minor(kf), key_minor(vf), key_minor(lf),
      *([ck] * group), *([cv] * group), *([cl] * group))
    o5 = o.reshape(nb, n_new, N_HEADS, N_HEADS, HEAD_DIM)
    return jnp.einsum('btggd->btgd', o5).reshape(nb * n_new, D_ATTN)


S5_COLS = 512


def _s5_prep_kernel(lre_ref, lim_ref, ldt_ref, bre_ref, bim_ref, jp_ref,
                    lbr_ref, lbi_ref, bbr_ref, bbi_ref, pwr_ref, pwi_ref):
    lre = lre_ref[...]
    lim = lim_ref[...]
    dt = jnp.exp(ldt_ref[...])
    mag = jnp.exp(lre * dt)
    ang = lim * dt
    lbr = mag * jnp.cos(ang)
    lbi = mag * jnp.sin(ang)
    lbr_ref[...] = lbr
    lbi_ref[...] = lbi
    nr = lbr - 1.0
    den = lre * lre + lim * lim
    cr = (nr * lre + lbi * lim) / den
    ci = (lbi * lre - nr * lim) / den
    bre = bre_ref[...]
    bim = bim_ref[...]
    bbr_ref[...] = cr * bre - ci * bim
    bbi_ref[...] = cr * bim + ci * bre
    jp = jp_ref[...]
    pmag = jnp.exp(jp * (lre * dt))
    pang = jp * ang
    pwr_ref[...] = pmag * jnp.cos(pang)
    pwi_ref[...] = pmag * jnp.sin(pang)


def _s5_prep(lam_re, lam_im, log_dt, b_re, b_im, chunk):
    lre = lam_re.reshape(1, N_STATE).astype(F32)
    lim = lam_im.reshape(1, N_STATE).astype(F32)
    ldt = jnp.repeat(log_dt.astype(F32), STATE_DIM).reshape(1, N_STATE)
    bre = jnp.transpose(b_re.astype(F32), (2, 0, 1)).reshape(SSM_GROUP, N_STATE)
    bim = jnp.transpose(b_im.astype(F32), (2, 0, 1)).reshape(SSM_GROUP, N_STATE)
    jp = (jnp.arange(chunk, dtype=jnp.int32) // SUBLANES + 1).astype(F32).reshape(chunk, 1)
    vec = jax.ShapeDtypeStruct((1, N_STATE), F32)
    mat = jax.ShapeDtypeStruct((SSM_GROUP, N_STATE), F32)
    tab = jax.ShapeDtypeStruct((chunk, N_STATE), F32)
    return pl.pallas_call(
        _s5_prep_kernel, name="s5_prep",
        out_shape=(vec, vec, mat, mat, tab, tab),
        compiler_params=pltpu.CompilerParams(vmem_limit_bytes=VMEM_LIMIT),
    )(lre, lim, ldt, bre, bim, jp)


def _s5_weights(bbr, bbi, c_re, c_im, d_skip, w_glu):
    eye = jnp.eye(N_GROUPS, dtype=F32)

    def in_map(bb):
        b3 = bb.reshape(SSM_GROUP, N_GROUPS, STATE_DIM)
        return jnp.einsum('cgp,gh->gchp', b3, eye).reshape(D_SSM, N_STATE).astype(BF16)

    def out_map(cc):
        return jnp.einsum('gcp,gh->gphc', cc.astype(F32), eye).reshape(N_STATE, D_SSM).astype(BF16)

    return (in_map(bbr), in_map(bbi), out_map(c_re), out_map(c_im),
            d_skip.astype(F32).reshape(1, D_SSM), w_glu.astype(BF16))


def _cmul_add(ar, ai, br, bi, cr, ci):
    return ar * br - ai * bi + cr, ar * bi + ai * br + ci


def _s5_body(u_ref, lbr_ref, lbi_ref, pwr_ref, pwi_ref, wbr_ref, wbi_ref, wcr_ref, wci_ref,
             dsk_ref, wg_ref, o_ref, hr_ref, hi_ref, tr_ref, ti_ref, start_states):
    chunk = u_ref.shape[1]
    seg = chunk // SUBLANES
    half_in = D_SSM // 2
    half_st = N_STATE // 2
    u = u_ref[0]
    ub = u.astype(BF16)
    for wref, href in ((wbr_ref, hr_ref), (wbi_ref, hi_ref)):
        href[:, :half_st] = jnp.dot(ub[:, :half_in], wref[:half_in, :half_st],
                                    preferred_element_type=F32)
        href[:, half_st:] = jnp.dot(ub[:, half_in:], wref[half_in:, half_st:],
                                    preferred_element_type=F32)

    ends_r, ends_i = [], []
    for cb in range(N_STATE // S5_COLS):
        cs = slice(cb * S5_COLS, (cb + 1) * S5_COLS)
        lr = jnp.broadcast_to(lbr_ref[:, cs], (SUBLANES, S5_COLS))
        li = jnp.broadcast_to(lbi_ref[:, cs], (SUBLANES, S5_COLS))

        def scan_step(j, carry):
            sr, si = carry
            rows = pl.ds(pl.multiple_of(j * SUBLANES, SUBLANES), SUBLANES)
            sr, si = _cmul_add(lr, li, sr, si, hr_ref[rows, cs], hi_ref[rows, cs])
            hr_ref[rows, cs] = sr
            hi_ref[rows, cs] = si
            return sr, si

        zero = jnp.zeros((SUBLANES, S5_COLS), F32)
        er, ei = lax.fori_loop(0, seg, scan_step, (zero, zero), unroll=min(seg, 4))
        ends_r.append(er)
        ends_i.append(ei)
    end_r = jnp.concatenate(ends_r, axis=1)
    end_i = jnp.concatenate(ends_i, axis=1)
    start_states(end_r, end_i)

    for cb in range(N_STATE // S5_COLS):
        cs = slice(cb * S5_COLS, (cb + 1) * S5_COLS)
        tr = tr_ref[:, cs]
        ti = ti_ref[:, cs]

        def fix_step(j, carry):
            rows = pl.ds(pl.multiple_of(j * SUBLANES, SUBLANES), SUBLANES)
            nr, ni = _cmul_add(pwr_ref[rows, cs], pwi_ref[rows, cs], tr, ti,
                               hr_ref[rows, cs], hi_ref[rows, cs])
            hr_ref[rows, cs] = nr
            hi_ref[rows, cs] = ni
            return carry

        lax.fori_loop(0, seg, fix_step, 0, unroll=min(seg, 4))

    half_out = D_SSM // 2
    ys = []
    for k in range(2):
        st = slice(k * half_st, (k + 1) * half_st)
        oc = slice(k * half_out, (k + 1) * half_out)
        ys.append(jnp.dot(hr_ref[:, st].astype(BF16), wcr_ref[st, oc], preferred_element_type=F32)
                  - jnp.dot(hi_ref[:, st].astype(BF16), wci_ref[st, oc], preferred_element_type=F32))
    y = jnp.concatenate(ys, axis=1) + dsk_ref[...] * u
    z = jax.nn.gelu(y, approximate=True)
    o_ref[0] = z * jax.nn.sigmoid(jnp.dot(z.astype(BF16), wg_ref[...], preferred_element_type=F32))


def _s5_seq_kernel(u_ref, lbr_ref, lbi_ref, pwr_ref, pwi_ref, wbr_ref, wbi_ref, wcr_ref, wci_ref,
                   dsk_ref, wg_ref, o_ref, hlr_ref, hli_ref,
                   hr_ref, hi_ref, tr_ref, ti_ref, cr_ref, ci_ref):
    chunk = u_ref.shape[1]

    @pl.when(pl.program_id(1) == 0)
    def _():
        cr_ref[...] = jnp.zeros_like(cr_ref)
        ci_ref[...] = jnp.zeros_like(ci_ref)

    def start_states(end_r, end_i):
        last = chunk - SUBLANES
        lsr = pwr_ref[last:last + 1, :]
        lsi = pwi_ref[last:last + 1, :]
        sr = cr_ref[...]
        si = ci_ref[...]
        for s in range(SUBLANES):
            tr_ref[s:s + 1, :] = sr
            ti_ref[s:s + 1, :] = si
            sr, si = _cmul_add(lsr, lsi, sr, si, end_r[s:s + 1, :], end_i[s:s + 1, :])
        cr_ref[...] = sr
        ci_ref[...] = si

    _s5_body(u_ref, lbr_ref, lbi_ref, pwr_ref, pwi_ref, wbr_ref, wbi_ref, wcr_ref, wci_ref,
             dsk_ref, wg_ref, o_ref, hr_ref, hi_ref, tr_ref, ti_ref, start_states)
    hlr_ref[0] = cr_ref[...]
    hli_ref[0] = ci_ref[...]


def _s5_batch_kernel(u_ref, h0r_ref, h0i_ref, lbr_ref, lbi_ref, pwr_ref, pwi_ref, wbr_ref, wbi_ref,
                     wcr_ref, wci_ref, dsk_ref, wg_ref, o_ref, hlr_ref, hli_ref,
                     hr_ref, hi_ref, tr_ref, ti_ref):
    chunk = u_ref.shape[1]

    def start_states(end_r, end_i):
        last = chunk - SUBLANES
        tr = h0r_ref[0]
        ti = h0i_ref[0]
        tr_ref[...] = tr
        ti_ref[...] = ti
        fr, fi = _cmul_add(pwr_ref[last:last + SUBLANES, :], pwi_ref[last:last + SUBLANES, :],
                           tr, ti, end_r, end_i)
        hlr_ref[0] = fr
        hli_ref[0] = fi

    _s5_body(u_ref, lbr_ref, lbi_ref, pwr_ref, pwi_ref, wbr_ref, wbi_ref, wcr_ref, wci_ref,
             dsk_ref, wg_ref, o_ref, hr_ref, hi_ref, tr_ref, ti_ref, start_states)


def _to_segment_rows(u, chunk):
    *lead, s, d = u.shape
    seg = chunk // SUBLANES
    x = u.reshape(*lead, s // chunk, SUBLANES, seg, d)
    return jnp.swapaxes(x, -2, -3).reshape(*lead, s, d)


def _from_segment_rows(u, chunk):
    *lead, s, d = u.shape
    seg = chunk // SUBLANES
    x = u.reshape(*lead, s // chunk, seg, SUBLANES, d)
    return jnp.swapaxes(x, -2, -3).reshape(*lead, s, d)


def _s5_scratch(chunk):
    return [pltpu.VMEM((chunk, N_STATE), F32), pltpu.VMEM((chunk, N_STATE), F32),
            pltpu.VMEM((SUBLANES, N_STATE), F32), pltpu.VMEM((SUBLANES, N_STATE), F32)]


def _s5_seq(u, prep, weights, chunk):
    n, s, d = u.shape
    lbr, lbi, _, _, pwr, pwi = prep
    consts = (lbr, lbi, pwr, pwi) + tuple(weights)
    state = jax.ShapeDtypeStruct((n, 1, N_STATE), F32)
    return pl.pallas_call(
        _s5_seq_kernel, name="s5_seq",
        grid=(n, s // chunk),
        in_specs=[pl.BlockSpec((1, chunk, d), lambda b, i: (b, i, 0))]
                 + [_const_spec(a.shape) for a in consts],
        out_specs=(pl.BlockSpec((1, chunk, d), lambda b, i: (b, i, 0)),
                   pl.BlockSpec((1, 1, N_STATE), lambda b, i: (b, 0, 0)),
                   pl.BlockSpec((1, 1, N_STATE), lambda b, i: (b, 0, 0))),
        out_shape=(jax.ShapeDtypeStruct((n, s, d), F32), state, state),
        scratch_shapes=_s5_scratch(chunk) + [pltpu.VMEM((1, N_STATE), F32), pltpu.VMEM((1, N_STATE), F32)],
        compiler_params=_cparams(("arbitrary", "arbitrary")),
    )(u, *consts)


def _s5_batch(u, h0r, h0i, prep, weights):
    nblk, chunk, d = u.shape
    lbr, lbi, _, _, pwr, pwi = prep
    consts = (lbr, lbi, pwr, pwi) + tuple(weights)
    state = jax.ShapeDtypeStruct((nblk, SUBLANES, N_STATE), F32)
    blk = lambda shape: pl.BlockSpec(shape, lambda i: (i, 0, 0))
    return pl.pallas_call(
        _s5_batch_kernel, name="s5_batch",
        grid=(nblk,),
        in_specs=[blk((1, chunk, d)), blk((1, SUBLANES, N_STATE)), blk((1, SUBLANES, N_STATE))]
                 + [_const_spec(a.shape) for a in consts],
        out_specs=(blk((1, chunk, d)), blk((1, SUBLANES, N_STATE)), blk((1, SUBLANES, N_STATE))),
        out_shape=(jax.ShapeDtypeStruct((nblk, chunk, d), F32), state, state),
        scratch_shapes=_s5_scratch(chunk),
        compiler_params=_cparams(("arbitrary",)),
    )(u, h0r, h0i, *consts)


_PEER_CELLS = [(i, j) for i in range(PEER_TOPK) for j in range(PEER_TOPK)
               if (i + 1) * (j + 1) <= PEER_TOPK]
_N_CELLS_PAD = -(-len(_PEER_CELLS) // SUBLANES) * SUBLANES
_BIG_ID = 1 << 20


def _pop_max(vals, ids):
    level = list(zip(vals, ids))
    while len(level) > 1:
        nxt = []
        for k in range(0, len(level) - 1, 2):
            (a, ia), (b, ib) = level[k], level[k + 1]
            take = a >= b
            nxt.append((jnp.where(take, a, b), jnp.where(take, ia, ib)))
        if len(level) % 2:
            nxt.append(level[-1])
        level = nxt
    v8, i8 = level[0]
    m = jnp.max(v8, axis=0, keepdims=True)
    idx = jnp.min(jnp.where(v8 == m, i8, _BIG_ID), axis=0, keepdims=True)
    return m, idx, [jnp.where(i == idx, NEG_INF, v) for v, i in zip(vals, ids)]


def _peer_route_kernel(xn_ref, wq_ref, keys_ref, cell_ref, a_ref, b_ref, w_ref,
                       q_ref, sv_ref, si_ref, cand_ref, cv_ref, fid_ref, ao_ref, bo_ref, wo_ref):
    t = xn_ref.shape[0]
    q_ref[...] = jnp.dot(xn_ref[...], wq_ref[...], preferred_element_type=F32)
    sub = lax.broadcasted_iota(jnp.int32, (SUBLANES, t), 0)
    key_ids = [sub + v * SUBLANES for v in range(N_KEYS // SUBLANES)]

    def slabs(x):
        return [x[v * SUBLANES:(v + 1) * SUBLANES, :] for v in range(x.shape[0] // SUBLANES)]

    def sub_topk(ht, carry):
        qh = q_ref[:, pl.ds(pl.multiple_of(ht * D_HALF, D_HALF), D_HALF)].astype(BF16)
        s = slabs(lax.dot_general(keys_ref[ht], qh, (((1,), (1,)), ((), ())),
                                  preferred_element_type=F32))
        for r in range(PEER_TOPK):
            m, idx, s = _pop_max(s, key_ids)
            sv_ref[ht, r:r + 1, :] = m
            si_ref[ht, r:r + 1, :] = idx
        return carry

    lax.fori_loop(0, 2 * PEER_HEADS, sub_topk, 0)

    cell_ids = slabs(cell_ref[...])

    def combine(h, carry):
        va = sv_ref[2 * h]
        vb = sv_ref[2 * h + 1]
        cand_ref[...] = jnp.full((_N_CELLS_PAD, t), NEG_INF, F32)
        for c, (i, j) in enumerate(_PEER_CELLS):
            cand_ref[c:c + 1, :] = va[i:i + 1, :] + vb[j:j + 1, :]
        cand = slabs(cand_ref[...])
        for r in range(PEER_TOPK):
            m, fid, cand = _pop_max(cand, cell_ids)
            cv_ref[r:r + 1, :] = m
            fid_ref[r:r + 1, :] = fid
        cv = cv_ref[...]
        fid = fid_ref[...]
        isel = lax.shift_right_logical(fid, 4)
        jsel = jnp.bitwise_and(fid, PEER_TOPK - 1)
        ia = si_ref[2 * h]
        ib = si_ref[2 * h + 1]
        ka = jnp.zeros((PEER_TOPK, t), jnp.int32)
        kb = jnp.zeros((PEER_TOPK, t), jnp.int32)
        for r in range(PEER_TOPK):
            ka = jnp.where(isel == r, ia[r:r + 1, :], ka)
            kb = jnp.where(jsel == r, ib[r:r + 1, :], kb)
        e = jnp.exp(cv - cv[0:1, :])
        ao_ref[h] = ka
        bo_ref[h] = kb
        wo_ref[h] = e / jnp.sum(e, axis=0, keepdims=True)
        return carry

    lax.fori_loop(0, PEER_HEADS, combine, 0)
    nk = PEER_HEADS * PEER_TOPK
    a_ref[...] = ao_ref[...].reshape(nk, t).T
    b_ref[...] = bo_ref[...].reshape(nk, t).T
    w_ref[...] = wo_ref[...].reshape(nk, t).T


def _peer_route(xn, wq, keys, t):
    n, d = xn.shape
    nk = PEER_HEADS * PEER_TOPK
    ids = [i * PEER_TOPK + j for i, j in _PEER_CELLS] + [_BIG_ID] * (_N_CELLS_PAD - len(_PEER_CELLS))
    cell = jnp.broadcast_to(jnp.asarray(ids, jnp.int32)[:, None], (_N_CELLS_PAD, t))
    tok = pl.BlockSpec((t, nk), lambda i: (i, 0))
    return pl.pallas_call(
        _peer_route_kernel, name="peer_route",
        grid=(n // t,),
        in_specs=[pl.BlockSpec((t, d), lambda i: (i, 0)), _const_spec(wq.shape),
                  _const_spec(keys.shape), _const_spec(cell.shape)],
        out_specs=(tok, tok, tok),
        out_shape=(jax.ShapeDtypeStruct((n, nk), jnp.int32), jax.ShapeDtypeStruct((n, nk), jnp.int32),
                   jax.ShapeDtypeStruct((n, nk), F32)),
        scratch_shapes=[pltpu.VMEM((t, wq.shape[1]), F32),
                        pltpu.VMEM((2 * PEER_HEADS, PEER_TOPK, t), F32),
                        pltpu.VMEM((2 * PEER_HEADS, PEER_TOPK, t), jnp.int32),
                        pltpu.VMEM((_N_CELLS_PAD, t), F32),
                        pltpu.VMEM((PEER_TOPK, t), F32), pltpu.VMEM((PEER_TOPK, t), jnp.int32),
                        pltpu.VMEM((PEER_HEADS, PEER_TOPK, t), jnp.int32),
                        pltpu.VMEM((PEER_HEADS, PEER_TOPK, t), jnp.int32),
                        pltpu.VMEM((PEER_HEADS, PEER_TOPK, t), F32)],
        compiler_params=_cparams(("arbitrary",)),
    )(xn, wq, keys, cell)


EXPERT_BLOCK = 2048
COEF_PITCH = 72
HALF_GROUPS = N_KEYS // 2


def _peer_hidden_kernel(xn_ref, a_ref, b_ref, w_ref, u_ref, c_ref, hsel_ref):
    j = pl.program_id(1)
    groups = u_ref.shape[0] // N_KEYS

    @pl.when(j == 0)
    def _():
        hsel_ref[...] = jnp.zeros_like(hsel_ref)

    h = lax.dot_general(xn_ref[...], u_ref[...], (((1,), (1,)), ((), ())),
                        preferred_element_type=F32)
    a_idx = a_ref[...]
    b_idx = b_ref[...]
    hs = hsel_ref[...]
    for g in range(groups):
        hg = h[:, g * N_KEYS:(g + 1) * N_KEYS]
        hs = jnp.where(a_idx == j * groups + g, jnp.take_along_axis(hg, b_idx, axis=1), hs)
    hsel_ref[...] = hs

    @pl.when(j == pl.num_programs(1) - 1)
    def _():
        c_ref[...] = w_ref[...] * jax.nn.gelu(hs, approximate=True)


def _peer_hidden(xn, a_idx, b_idx, w, u_bf16, t):
    n, d = xn.shape
    nk = a_idx.shape[1]
    ne = u_bf16.shape[0]
    tok = lambda width: pl.BlockSpec((t, width), lambda i, j: (i, 0))
    return pl.pallas_call(
        _peer_hidden_kernel, name="peer_hidden",
        grid=(n // t, ne // EXPERT_BLOCK),
        in_specs=[tok(d), tok(nk), tok(nk), tok(nk),
                  pl.BlockSpec((EXPERT_BLOCK, d), lambda i, j: (j, 0))],
        out_specs=tok(nk),
        out_shape=jax.ShapeDtypeStruct((n, nk), F32),
        scratch_shapes=[pltpu.VMEM((t, nk), F32)],
        compiler_params=_cparams(("arbitrary", "arbitrary")),
    )(xn, a_idx, b_idx, w, u_bf16)


def _peer_out_kernel(a_ref, b_ref, c_ref, x1_ref, v_ref, y_ref, s_ref, acc_ref):
    j = pl.program_id(1)
    t = a_ref.shape[0]
    pairs = v_ref.shape[0] // (2 * N_KEYS)
    hi_mask = jnp.uint32(0xFFFF0000)

    @pl.when(j == 0)
    def _():
        sub = lax.broadcasted_iota(jnp.int32, (N_KEYS, N_KEYS), 0)

        def one_token(i, carry):
            arow = a_ref[pl.ds(i, 1), :]
            brow = b_ref[pl.ds(i, 1), :]
            crow = c_ref[pl.ds(i, 1), :]
            pt = jnp.where(sub == arow, 1.0, 0.0).astype(BF16)
            rt = jnp.where(sub == brow, crow, 0.0).astype(BF16)
            coef = lax.dot_general(pt, rt, (((1,), (1,)), ((), ())), preferred_element_type=F32)
            bits = pltpu.bitcast(coef.astype(BF16).astype(F32), jnp.uint32)
            packed = jnp.bitwise_or(jnp.bitwise_and(bits[:HALF_GROUPS], hi_mask),
                                    lax.shift_right_logical(bits[HALF_GROUPS:], jnp.uint32(16)))
            s_ref[pl.ds(pl.multiple_of(i * COEF_PITCH, SUBLANES), HALF_GROUPS), :] = packed
            return carry

        lax.fori_loop(0, t, one_token, 0, unroll=32)

    parts = []
    for g in range(pairs):
        w32 = s_ref[pl.ds(j * pairs + g, t, stride=COEF_PITCH), :]
        parts.append(pltpu.bitcast(jnp.bitwise_and(w32, hi_mask), F32).astype(BF16))
        parts.append(pltpu.bitcast(lax.shift_left(w32, jnp.uint32(16)), F32).astype(BF16))
    coef_blk = jnp.concatenate(parts, axis=1)
    contrib = jnp.dot(coef_blk, v_ref[...], preferred_element_type=F32)

    @pl.when(j == 0)
    def _():
        acc_ref[...] = contrib

    @pl.when(j > 0)
    def _():
        acc_ref[...] += contrib

    @pl.when(j == pl.num_programs(1) - 1)
    def _():
        y_ref[...] = x1_ref[...] + acc_ref[...]


def _pair_groups(table_bf16):
    ne, d = table_bf16.shape
    x = table_bf16.reshape(2, HALF_GROUPS, N_KEYS, d)
    return jnp.swapaxes(x, 0, 1).reshape(ne, d)


def _peer_out(a_idx, b_idx, c, x1, v_paired, t):
    n, d = x1.shape
    nk = a_idx.shape[1]
    ne = v_paired.shape[0]
    tok = lambda width: pl.BlockSpec((t, width), lambda i, j: (i, 0))
    return pl.pallas_call(
        _peer_out_kernel, name="peer_out",
        grid=(n // t, ne // EXPERT_BLOCK),
        in_specs=[tok(nk), tok(nk), tok(nk), tok(d),
                  pl.BlockSpec((EXPERT_BLOCK, d), lambda i, j: (j, 0))],
        out_specs=tok(d),
        out_shape=jax.ShapeDtypeStruct((n, d), F32),
        scratch_shapes=[pltpu.VMEM((t * COEF_PITCH, N_KEYS), jnp.uint32), pltpu.VMEM((t, d), F32)],
        compiler_params=_cparams(("arbitrary", "arbitrary")),
    )(a_idx, b_idx, c, x1, v_paired)


TOKEN_TILE = 512
ATTN_TILE = 1024
S5_CHUNK = 256
ROUTE_TILE = 512
EXPERT_TILE = 512


def _tile(n, pref):
    t = min(pref, n)
    while n % t:
        t //= 2
    return t


def _tail(x1, xn2, lw):
    n = x1.shape[0]
    a_idx, b_idx, w = _peer_route(xn2, lw['wq'], lw['keys'], _tile(n, ROUTE_TILE))
    te = _tile(n, EXPERT_TILE)
    c = _peer_hidden(xn2, a_idx, b_idx, w, lw['u'], te)
    return _peer_out(a_idx, b_idx, c, x1, lw['v'], te)


def _layer_weights(l, norm1_g, w_in, b_forget, q_norm_g, k_norm_g, ssm_lam_re, ssm_lam_im, ssm_log_dt,
                   ssm_b_re, ssm_b_im, ssm_c_re, ssm_c_im, ssm_d, w_glu, w_proj_attn, w_proj_ssm, w_out,
                   norm2_g, w_query, sub_keys, expert_u, expert_v, chunks):
    lw = {'inproj': _prep_inproj(norm1_g[l], w_in[l], b_forget[l], q_norm_g[l], k_norm_g[l])}
    lw['s5_prep'] = {c: _s5_prep(ssm_lam_re[l], ssm_lam_im[l], ssm_log_dt[l], ssm_b_re[l], ssm_b_im[l], c)
                     for c in chunks}
    p0 = lw['s5_prep'][chunks[0]]
    lw['s5_w'] = _s5_weights(p0[2], p0[3], ssm_c_re[l], ssm_c_im[l], ssm_d[l], w_glu[l])
    lw['mix'] = (w_proj_attn[l].astype(BF16), w_proj_ssm[l].astype(BF16), w_out[l].astype(BF16),
                 norm2_g[l].astype(F32)[None, :])
    lw['wq'] = w_query[l].astype(BF16)
    lw['keys'] = sub_keys[l].reshape(2 * PEER_HEADS, N_KEYS, D_HALF).astype(BF16)
    lw['u'] = expert_u[l].astype(BF16)
    lw['v'] = _pair_groups(expert_v[l].astype(BF16))
    return lw


def kernel(x_prompt, x_sample, cache_k, cache_v, cache_logf, state_ssm_re, state_ssm_im, page_table,
           norm1_g, w_in, b_forget, q_norm_g, k_norm_g,
           ssm_lam_re, ssm_lam_im, ssm_log_dt, ssm_b_re, ssm_b_im, ssm_c_re, ssm_c_im, ssm_d, w_glu,
           w_proj_attn, w_proj_ssm, w_out, norm2_g, w_query, sub_keys, expert_u, expert_v):
    nb, s, d = x_prompt.shape
    db, t, _ = x_sample.shape
    depth = w_in.shape[0]
    assert db % SUBLANES == 0 and t * N_HEADS <= LANES and t <= PAGE
    batch_chunk = SUBLANES * t
    y_p = x_prompt.reshape(nb * s, d)
    y_s = x_sample.reshape(db * t, d)
    outs = [[] for _ in range(10)]
    for l in range(depth):
        lw = _layer_weights(l, norm1_g, w_in, b_forget, q_norm_g, k_norm_g, ssm_lam_re, ssm_lam_im,
                            ssm_log_dt, ssm_b_re, ssm_b_im, ssm_c_re, ssm_c_im, ssm_d, w_glu,
                            w_proj_attn, w_proj_ssm, w_out, norm2_g, w_query, sub_keys, expert_u,
                            expert_v, (S5_CHUNK, batch_chunk))

        tm = _tile(nb * s, TOKEN_TILE)
        qb, kb, vb, kf, vf, lf, u, ga, gs = _inproj(y_p, *lw['inproj'], tm=tm, seq_len=s)
        token_major = lambda x: jnp.transpose(x.reshape(nb, N_HEADS, HEAD_DIM, s), (0, 3, 1, 2))
        lf3 = lf.reshape(nb, s, N_HEADS)
        ta = _tile(s, ATTN_TILE)
        ct = _cumsum(jnp.swapaxes(lf3, 1, 2), ta)
        o_attn = _attn_prompt(qb.reshape(nb, s, D_ATTN), kb.reshape(nb, s, D_ATTN),
                              vb.reshape(nb, s, D_ATTN), ct, ta)
        u_seg = _to_segment_rows(u.reshape(nb, s, D_SSM), S5_CHUNK)
        o_ssm, hr, hi = _s5_seq(u_seg, lw['s5_prep'][S5_CHUNK], lw['s5_w'], S5_CHUNK)
        o_ssm = _from_segment_rows(o_ssm, S5_CHUNK).reshape(nb * s, D_SSM)
        x1, xn2 = _mix(o_attn.reshape(nb * s, D_ATTN), o_ssm, ga, gs, y_p, *lw['mix'], tm=tm)
        y_p = _tail(x1, xn2, lw)
        for slot, val in zip(range(5), (token_major(kf), token_major(vf), lf3,
                                        hr.reshape(nb, N_GROUPS, STATE_DIM),
                                        hi.reshape(nb, N_GROUPS, STATE_DIM))):
            outs[slot].append(val)

        tm = _tile(db * t, TOKEN_TILE)
        qb, kb, vb, kf, vf, lf, u, ga, gs = _inproj(y_s, *lw['inproj'], tm=tm)
        o_attn = _attn_paged(page_table, qb, kf, vf, lf, cache_k, cache_v, cache_logf, t, l).astype(BF16)
        nblk = db // SUBLANES
        u_seg = _to_segment_rows(u.reshape(nblk, batch_chunk, D_SSM), batch_chunk)
        h0r = state_ssm_re[l].astype(F32).reshape(nblk, SUBLANES, N_STATE)
        h0i = state_ssm_im[l].astype(F32).reshape(nblk, SUBLANES, N_STATE)
        o_ssm, hr, hi = _s5_batch(u_seg, h0r, h0i, lw['s5_prep'][batch_chunk], lw['s5_w'])
        o_ssm = _from_segment_rows(o_ssm, batch_chunk).reshape(db * t, D_SSM)
        x1, xn2 = _mix(o_attn, o_ssm, ga, gs, y_s, *lw['mix'], tm=tm)
        y_s = _tail(x1, xn2, lw)
        for slot, val in zip(range(5, 10), (kf.reshape(db, t, N_HEADS, HEAD_DIM),
                                            vf.reshape(db, t, N_HEADS, HEAD_DIM),
                                            lf.reshape(db, t, N_HEADS),
                                            hr.reshape(db, N_GROUPS, STATE_DIM),
                                            hi.reshape(db, N_GROUPS, STATE_DIM))):
            outs[slot].append(val)
    stacked = [jnp.stack(o) for o in outs]
    return (y_p.reshape(nb, s, d), y_s.reshape(db, t, d), *stacked)
```

```python
import functools
import math

import jax
import jax.numpy as jnp
from jax import lax
from jax.experimental import pallas as pl
from jax.experimental.pallas import tpu as pltpu

F32 = jnp.float32
BF16 = jnp.bfloat16

N_HEADS = 8
HEAD_DIM = 64
D_ATTN = N_HEADS * HEAD_DIM
ATTN_SCALE = HEAD_DIM ** -0.5
LOG2E = math.log2(math.e)
D_SSM = 512
SSM_GROUP = 16
N_GROUPS = D_SSM // SSM_GROUP
STATE_DIM = 64
N_STATE = N_GROUPS * STATE_DIM
PEER_HEADS = 8
N_KEYS = 128
PEER_TOPK = 16
D_HALF = 128
NORM_EPS = 1e-6
NEG_INF = -1e30
PAGE = 128

LANES = 128
SUBLANES = 8
VMEM_LIMIT = 56 * 1024 * 1024


def _cparams(sem):
    return pltpu.CompilerParams(dimension_semantics=sem, vmem_limit_bytes=VMEM_LIMIT)


def _const_spec(shape):
    nd = len(shape)
    return pl.BlockSpec(shape, lambda *_: (0,) * nd)


def _rms_rows(x, g):
    r = lax.rsqrt(jnp.mean(x * x, axis=-1, keepdims=True) + NORM_EPS)
    return (x * r) * g


def _split_dot(a, b_bf16):
    hi = a.astype(BF16)
    lo = (a - hi.astype(F32)).astype(BF16)
    return (jnp.dot(hi, b_bf16, preferred_element_type=F32)
            + jnp.dot(lo, b_bf16, preferred_element_type=F32))


def _log_sigmoid(x):
    return jnp.minimum(x, 0.0) - jnp.log1p(jnp.exp(-jnp.abs(x)))


def _inproj_kernel(x_ref, g1_ref, w_ref, wf_ref, bf_ref, gq_ref, gk_ref, hsel_ref,
                   qb_ref, kb_ref, vb_ref, kf_ref, vf_ref, lf_ref, u_ref, ga_ref, gs_ref, *, key_minor):
    xn = _rms_rows(x_ref[...], g1_ref[...]).astype(BF16)

    def seg(i, n):
        return jnp.dot(xn, w_ref[:, i:i + n], preferred_element_type=F32)

    hsel = hsel_ref[...]

    def head_norm(z, g):
        ms = _split_dot(z * z, hsel)
        return (z * lax.rsqrt(ms + NORM_EPS)) * g

    q = head_norm(seg(0, D_ATTN), gq_ref[...])
    qb_ref[...] = (q * (ATTN_SCALE * LOG2E)).astype(BF16)
    k = head_norm(seg(D_ATTN, D_ATTN), gk_ref[...])
    kb_ref[...] = k.astype(BF16)
    v = seg(2 * D_ATTN, D_ATTN)
    vb_ref[...] = v.astype(BF16)
    if key_minor:
        kf_ref[0] = k.T
        vf_ref[0] = v.T
    else:
        kf_ref[...] = k
        vf_ref[...] = v
    u_ref[...] = seg(3 * D_ATTN, D_SSM)
    o = 3 * D_ATTN + D_SSM
    d = ga_ref.shape[1]
    ga_ref[...] = jax.nn.sigmoid(seg(o, d)).astype(BF16)
    gs_ref[...] = jax.nn.sigmoid(seg(o + d, d)).astype(BF16)
    zf = jnp.dot(xn, wf_ref[...], preferred_element_type=F32) + bf_ref[...]
    lf_ref[...] = _log_sigmoid(zf)[:, :N_HEADS]


def _prep_inproj(norm1_g, w_in, b_forget, q_norm_g, k_norm_g):
    o3 = 3 * D_ATTN
    o4 = o3 + N_HEADS
    wmain = jnp.concatenate([w_in[:, :o3], w_in[:, o4:]], axis=1).astype(BF16)
    wf = jnp.pad(w_in[:, o3:o4], ((0, 0), (0, LANES - N_HEADS))).astype(BF16)
    bfp = jnp.pad(b_forget.astype(F32), (0, LANES - N_HEADS))[None, :]
    gq = jnp.tile(q_norm_g.astype(F32), N_HEADS)[None, :]
    gk = jnp.tile(k_norm_g.astype(F32), N_HEADS)[None, :]
    hsel = jnp.kron(jnp.eye(N_HEADS, dtype=F32),
                    jnp.full((HEAD_DIM, HEAD_DIM), 1.0 / HEAD_DIM, F32)).astype(BF16)
    return norm1_g.astype(F32)[None, :], wmain, wf, bfp, gq, gk, hsel


def _inproj(x, g1, wmain, wf, bfp, gq, gk, hsel, tm, seq_len=None):
    n, d = x.shape
    tok = lambda w: pl.BlockSpec((tm, w), lambda i: (i, 0))
    if seq_len is None:
        kv_shape = jax.ShapeDtypeStruct((n, D_ATTN), F32)
        kv_spec = tok(D_ATTN)
    else:
        per_seq = seq_len // tm
        kv_shape = jax.ShapeDtypeStruct((n // seq_len, D_ATTN, seq_len), F32)
        kv_spec = pl.BlockSpec((1, D_ATTN, tm), lambda i: (i // per_seq, 0, i % per_seq))
    out_shape = (
        jax.ShapeDtypeStruct((n, D_ATTN), BF16), jax.ShapeDtypeStruct((n, D_ATTN), BF16),
        jax.ShapeDtypeStruct((n, D_ATTN), BF16), kv_shape, kv_shape,
        jax.ShapeDtypeStruct((n, N_HEADS), F32),
        jax.ShapeDtypeStruct((n, D_SSM), F32), jax.ShapeDtypeStruct((n, d), BF16),
        jax.ShapeDtypeStruct((n, d), BF16))
    return pl.pallas_call(
        functools.partial(_inproj_kernel, key_minor=seq_len is not None), name="inproj",
        grid=(n // tm,),
        in_specs=[tok(d), _const_spec(g1.shape), _const_spec(wmain.shape), _const_spec(wf.shape),
                  _const_spec(bfp.shape), _const_spec(gq.shape), _const_spec(gk.shape),
                  _const_spec(hsel.shape)],
        out_specs=(tok(D_ATTN), tok(D_ATTN), tok(D_ATTN), kv_spec, kv_spec, tok(N_HEADS),
                   tok(D_SSM), tok(d), tok(d)),
        out_shape=out_shape,
        compiler_params=_cparams(("arbitrary",)),
    )(x, g1, wmain, wf, bfp, gq, gk, hsel)


def _mix_kernel(oa_ref, os_ref, ga_ref, gs_ref, x_ref, wpa_ref, wps_ref, wo_ref, g2_ref,
                x1_ref, xn_ref):
    pa = jnp.dot(oa_ref[...], wpa_ref[...], preferred_element_type=F32)
    ps = jnp.dot(os_ref[...].astype(BF16), wps_ref[...], preferred_element_type=F32)
    mixed = ga_ref[...].astype(F32) * pa + gs_ref[...].astype(F32) * ps
    x1 = x_ref[...] + jnp.dot(mixed.astype(BF16), wo_ref[...], preferred_element_type=F32)
    x1_ref[...] = x1
    xn_ref[...] = _rms_rows(x1, g2_ref[...]).astype(BF16)


def _mix(oa, os_, ga, gs, x, wpa, wps, wo, g2, tm):
    n, d = x.shape
    tok = lambda w: pl.BlockSpec((tm, w), lambda i: (i, 0))
    return pl.pallas_call(
        _mix_kernel, name="mix",
        grid=(n // tm,),
        in_specs=[tok(D_ATTN), tok(D_SSM), tok(d), tok(d), tok(d), _const_spec(wpa.shape),
                  _const_spec(wps.shape), _const_spec(wo.shape), _const_spec(g2.shape)],
        out_specs=(tok(d), tok(d)),
        out_shape=(jax.ShapeDtypeStruct((n, d), F32), jax.ShapeDtypeStruct((n, d), BF16)),
        compiler_params=_cparams(("arbitrary",)),
    )(oa, os_, ga, gs, x, wpa, wps, wo, g2)


def _split3(a):
    parts = []
    rem = a
    for _ in range(3):
        part = rem.astype(BF16)
        rem = rem - part.astype(F32)
        parts.append(part)
    return parts


def _dot01_right(a, b01):
    return sum(jnp.dot(p, b01, preferred_element_type=F32) for p in _split3(a))


def _cumsum_kernel(lft_ref, ct_ref, carry_ref):
    t = lft_ref.shape[2]

    @pl.when(pl.program_id(1) == 0)
    def _():
        carry_ref[...] = jnp.zeros_like(carry_ref)

    r = lax.broadcasted_iota(jnp.int32, (t, t), 0)
    c = lax.broadcasted_iota(jnp.int32, (t, t), 1)
    upto = jnp.where(r <= c, 1.0, 0.0).astype(BF16)
    cs = carry_ref[...] + _dot01_right(lft_ref[0], upto)
    ct_ref[0] = cs * LOG2E
    carry_ref[...] = cs[:, t - 1:t]


def _cumsum(lft, t):
    n, h, s = lft.shape
    return pl.pallas_call(
        _cumsum_kernel, name="logf_cumsum",
        grid=(n, s // t),
        in_specs=[pl.BlockSpec((1, h, t), lambda b, i: (b, 0, i))],
        out_specs=pl.BlockSpec((1, h, t), lambda b, i: (b, 0, i)),
        out_shape=jax.ShapeDtypeStruct((n, h, s), F32),
        scratch_shapes=[pltpu.VMEM((h, 1), F32)],
        compiler_params=_cparams(("arbitrary", "arbitrary")),
    )(lft)


ATTN_ROW_BLOCK = 256


def _attn_prompt_kernel(first_ref, q_ref, k_ref, v_ref, ct_ref, o_ref, m_ref, acc_ref,
                        s0_ref, s1_ref, p0_ref, p1_ref, a0_ref, a1_ref):
    tq = q_ref.shape[1]
    tk = k_ref.shape[1]
    rb_rows = min(ATTN_ROW_BLOCK, tq)
    n_rb = tq // rb_rows
    n_iter = (N_HEADS // 2) * n_rb
    qi = pl.program_id(1)
    ki = first_ref[pl.program_id(0), qi] + pl.program_id(2)
    s_refs = (s0_ref, s1_ref)
    p_refs = (p0_ref, p1_ref)
    a_refs = (a0_ref, a1_ref)

    @pl.when(pl.program_id(2) == 0)
    def _():
        m_ref[...] = jnp.full_like(m_ref, NEG_INF)
        acc_ref[...] = jnp.zeros_like(acc_ref)

    lane = lax.broadcasted_iota(jnp.int32, (tk, LANES), 1)
    keep = (jnp.where(lane < HEAD_DIM, 1.0, 0.0).astype(BF16),
            jnp.where(lane < HEAD_DIM, 0.0, 1.0).astype(BF16))
    ones = (jnp.where(lane == HEAD_DIM, 1.0, 0.0).astype(BF16),
            jnp.where(lane == 0, 1.0, 0.0).astype(BF16))

    def where_is(j):
        pair = j // n_rb
        return pair, pl.multiple_of((j - pair * n_rb) * rb_rows, rb_rows)

    def slab(pair):
        return pl.ds(pl.multiple_of(pair * LANES, LANES), LANES)

    def scores(j, half):
        pair, row0 = where_is(j)
        qblk = q_ref[0, pl.ds(row0, rb_rows), slab(pair)]
        kh = k_ref[0, :, slab(pair)] * keep[half]
        s_refs[half][...] = lax.dot_general(qblk, kh, (((1,), (1,)), ((), ())),
                                            preferred_element_type=F32)

    def softmax(j, half, diagonal):
        pair, row0 = where_is(j)
        h = 2 * pair + half
        ckb = jnp.broadcast_to(ct_ref[0, pl.ds(h, 1), :], (SUBLANES, tk))
        if diagonal:
            col = lax.broadcasted_iota(jnp.int32, (SUBLANES, tk), 1)
            sub = lax.broadcasted_iota(jnp.int32, (SUBLANES, tk), 0) + row0
        probs = None
        for g in range(rb_rows // SUBLANES):
            lo = g * SUBLANES
            s = s_refs[half][lo:lo + SUBLANES, :] - ckb
            if diagonal:
                s = jnp.where(col <= sub + lo, s, NEG_INF)
            rows = pl.ds(row0 + lo, SUBLANES)
            m_prev = m_ref[h, rows, :]
            m_new = jnp.maximum(m_prev, jnp.max(s, axis=-1, keepdims=True))
            a_refs[half][lo:lo + SUBLANES, :] = jnp.exp2(m_prev - m_new)
            m_ref[h, rows, :] = m_new
            prob = jnp.exp2(s - m_new)
            if g % 2 == 0:
                probs = prob
            else:
                p_refs[half][lo - SUBLANES:lo + SUBLANES, :] = (
                    jnp.concatenate([probs, prob], axis=0).astype(BF16))

    def values(j, half):
        pair, row0 = where_is(j)
        h = 2 * pair + half
        vh = v_ref[0, :, slab(pair)] * keep[half] + ones[half]
        pv = jnp.dot(p_refs[half][...], vh, preferred_element_type=F32)
        rows = pl.ds(row0, rb_rows)
        acc_ref[h, rows, :] = a_refs[half][...] * acc_ref[h, rows, :] + pv

    def step(diagonal):
        scores(0, 0)
        scores(0, 1)
        softmax(0, 0, diagonal)

        def body(j, carry):
            scores(j, 0)
            softmax(j - 1, 1, diagonal)
            values(j - 1, 0)
            scores(j, 1)
            softmax(j, 0, diagonal)
            values(j - 1, 1)
            return carry

        lax.fori_loop(1, n_iter, body, 0)
        softmax(n_iter - 1, 1, diagonal)
        values(n_iter - 1, 0)
        values(n_iter - 1, 1)

    @pl.when(ki < qi)
    def _():
        step(False)

    @pl.when(ki == qi)
    def _():
        step(True)
        lane_q = lax.broadcasted_iota(jnp.int32, (tq, LANES), 1) < HEAD_DIM
        for hp in range(N_HEADS // 2):
            lo = acc_ref[2 * hp]
            hi = acc_ref[2 * hp + 1]
            out = jnp.where(lane_q, lo / lo[:, HEAD_DIM:HEAD_DIM + 1], hi / hi[:, 0:1])
            o_ref[0, :, hp * LANES:(hp + 1) * LANES] = out.astype(o_ref.dtype)


UNDERFLOW_LOG2 = 160.0


def _first_key_block(qb, kb, ct, t):
    n, s, d = qb.shape
    nb = s // t

    def block_norm(x):
        x2 = jnp.square(x.astype(F32)).reshape(n, nb, t, N_HEADS, HEAD_DIM).sum(-1)
        return jnp.sqrt(x2.max(axis=2)) * 1.02
    qn = block_norm(qb)
    kn = block_norm(kb)
    c_start = jnp.swapaxes(ct[:, :, 0::t], 1, 2)
    c_end = jnp.swapaxes(ct[:, :, t - 1::t], 1, 2)
    bound = (qn[:, :, None] * kn[:, None, :] + (qn * kn)[:, :, None]
             + c_start[:, :, None] - c_end[:, None, :])
    needed = jnp.any(bound >= -UNDERFLOW_LOG2, axis=-1)
    i_idx = jnp.arange(nb)[:, None]
    j_idx = jnp.arange(nb)[None, :]
    needed = jnp.where(j_idx < i_idx, needed, j_idx == i_idx)
    return jnp.argmax(needed, axis=-1).astype(jnp.int32)


def _attn_prompt(qb, kb, vb, ct, t):
    n, s, d = qb.shape
    nb = s // t
    rb = min(ATTN_ROW_BLOCK, t)
    first = _first_key_block(qb, kb, ct, t)
    qmap = lambda b, i, j, first: (b, i, 0)
    kblock = lambda b, i, j, first: jnp.minimum(first[b, i] + j, i)
    kmap = lambda b, i, j, first: (b, kblock(b, i, j, first), 0)
    grid_spec = pltpu.PrefetchScalarGridSpec(
        num_scalar_prefetch=1,
        grid=(n, nb, nb),
        in_specs=[pl.BlockSpec((1, t, d), qmap), pl.BlockSpec((1, t, d), kmap),
                  pl.BlockSpec((1, t, d), kmap),
                  pl.BlockSpec((1, N_HEADS, t), lambda b, i, j, first: (b, 0, kblock(b, i, j, first)))],
        out_specs=pl.BlockSpec((1, t, d), qmap),
        scratch_shapes=[pltpu.VMEM((N_HEADS, t, 1), F32), pltpu.VMEM((N_HEADS, t, LANES), F32),
                        pltpu.VMEM((rb, t), F32), pltpu.VMEM((rb, t), F32),
                        pltpu.VMEM((rb, t), BF16), pltpu.VMEM((rb, t), BF16),
                        pltpu.VMEM((rb, 1), F32), pltpu.VMEM((rb, 1), F32)])
    return pl.pallas_call(
        _attn_prompt_kernel, name="attn_prompt",
        grid_spec=grid_spec,
        out_shape=jax.ShapeDtypeStruct((n, s, d), BF16),
        compiler_params=_cparams(("arbitrary", "arbitrary", "arbitrary")),
    )(first, qb, kb, vb, ct)


PAGES_PER_STEP = 32


def _attn_paged_kernel(*refs, n_new, group):
    pt_ref, q_ref, kn_ref, vn_ref, lfn_ref = refs[:5]
    kc_refs = refs[5:5 + group]
    vc_refs = refs[5 + group:5 + 2 * group]
    lfc_refs = refs[5 + 2 * group:5 + 3 * group]
    o_ref, m_ref, l_ref, acc_ref, carry_ref, cq_ref = refs[5 + 3 * group:]
    del pt_ref
    p = pl.program_id(1)
    rows = n_new * N_HEADS
    krow = lax.broadcasted_iota(jnp.int32, (PAGE, PAGE), 0)
    kcol = lax.broadcasted_iota(jnp.int32, (PAGE, PAGE), 1)
    later = jnp.where(krow > kcol, 1.0, 0.0).astype(BF16)
    qbd = q_ref[0]

    def suffix_in_page(lf_page):
        loc = _dot01_right(lf_page, later)
        return loc, loc[:, 0:1] + lf_page[:, 0:1]

    def visit(pages, carry, valid):
        scores, vals = [], []
        for k_page, v_page, lf_page in pages:
            loc, total = suffix_in_page(lf_page)
            bias = jnp.concatenate([(loc + carry) * LOG2E] * n_new, axis=0)
            s = jnp.dot(qbd, k_page.astype(BF16), preferred_element_type=F32) + bias
            scores.append(s)
            vals.append(v_page.astype(BF16))
            carry = carry + total
        s = jnp.concatenate(scores, axis=1)
        if valid is not None:
            s = jnp.where(valid, s, NEG_INF)
        cq = cq_ref[...]
        m_prev = m_ref[...]
        m_new = jnp.maximum(m_prev, jnp.max(s, axis=1, keepdims=True) + cq)
        prob = jnp.exp2(s - (m_new - cq))
        alpha = jnp.exp2(m_prev - m_new)
        l_ref[...] = alpha * l_ref[...] + jnp.sum(prob, axis=1, keepdims=True)
        m_ref[...] = m_new
        pb = prob.astype(BF16)
        pv = None
        for i, v in enumerate(vals):
            t = lax.dot_general(pb[:, i * PAGE:(i + 1) * PAGE], v, (((1,), (1,)), ((), ())),
                                preferred_element_type=F32)
            pv = t if pv is None else pv + t
        acc_ref[...] = alpha * acc_ref[...] + pv
        return carry

    @pl.when(p == 0)
    def _():
        row = lax.broadcasted_iota(jnp.int32, (rows, PAGE), 0)
        col = lax.broadcasted_iota(jnp.int32, (rows, PAGE), 1)
        qry = lax.shift_right_logical(row, 3)
        lf_new = lfn_ref[0]
        loc, _ = suffix_in_page(lf_new)
        tiled = jnp.concatenate([loc] * n_new, axis=0)
        cq_ref[...] = -LOG2E * jnp.sum(jnp.where(col == qry, tiled, 0.0), axis=1, keepdims=True)
        m_ref[...] = jnp.full_like(m_ref, NEG_INF)
        l_ref[...] = jnp.zeros_like(l_ref)
        acc_ref[...] = jnp.zeros_like(acc_ref)
        carry_ref[...] = visit([(kn_ref[0], vn_ref[0], lf_new)], jnp.zeros((N_HEADS, 1), F32),
                               col <= qry)

    @pl.when(p > 0)
    def _():
        pages = [(kc_refs[i][0, 0], vc_refs[i][0, 0], lfc_refs[i][0, 0]) for i in range(group)]
        carry_ref[...] = visit(pages, carry_ref[...], None)

    @pl.when(p == pl.num_programs(1) - 1)
    def _():
        o_ref[0] = acc_ref[...] / l_ref[...]


def _attn_paged(page_table, qb, kf, vf, lf, cache_k, cache_v, cache_logf, n_new, layer):
    nb, n_pages = page_table.shape
    rows = n_new * N_HEADS
    group = _tile(n_pages, PAGES_PER_STEP)
    ck = jnp.transpose(cache_k, (0, 1, 3, 4, 2)).reshape(cache_k.shape[0], -1, D_ATTN, PAGE)
    cv = jnp.transpose(cache_v, (0, 1, 3, 4, 2)).reshape(cache_v.shape[0], -1, D_ATTN, PAGE)
    cl = jnp.transpose(cache_logf, (0, 1, 3, 2))
    q4 = qb.reshape(nb, n_new, N_HEADS, HEAD_DIM)
    qbd = jnp.einsum('bthd,gh->btghd', q4, jnp.eye(N_HEADS, dtype=qb.dtype)).reshape(nb, rows, D_ATTN)
    key_minor = lambda x: jnp.pad(jnp.swapaxes(x.reshape(nb, n_new, -1), 1, 2),
                                  ((0, 0), (0, 0), (0, PAGE - n_new)))
    per_seq = lambda b, p, pt: (b, 0, 0)

    def page(i):
        return lambda b, p, pt: (layer, pt[b, n_pages - 1 - (jnp.maximum(p, 1) - 1) * group - i], 0, 0)

    in_specs = [pl.BlockSpec((1, rows, D_ATTN), per_seq), pl.BlockSpec((1, D_ATTN, PAGE), per_seq),
                pl.BlockSpec((1, D_ATTN, PAGE), per_seq), pl.BlockSpec((1, N_HEADS, PAGE), per_seq)]
    in_specs += [pl.BlockSpec((1, 1, D_ATTN, PAGE), page(i)) for i in range(group)] * 2
    in_specs += [pl.BlockSpec((1, 1, N_HEADS, PAGE), page(i)) for i in range(group)]
    grid_spec = pltpu.PrefetchScalarGridSpec(
        num_scalar_prefetch=1,
        grid=(nb, n_pages // group + 1),
        in_specs=in_specs,
        out_specs=pl.BlockSpec((1, rows, D_ATTN), per_seq),
        scratch_shapes=[pltpu.VMEM((rows, 1), F32), pltpu.VMEM((rows, 1), F32),
                        pltpu.VMEM((rows, D_ATTN), F32), pltpu.VMEM((N_HEADS, 1), F32),
                        pltpu.VMEM((rows, 1), F32)])
    o = pl.pallas_call(
        functools.partial(_attn_paged_kernel, n_new=n_new, group=group), name="attn_paged",
        grid_spec=grid_spec,
        out_shape=jax.ShapeDtypeStruct((nb, rows, D_ATTN), F32),
        compiler_params=_cparams(("arbitrary", "arbitrary")),
    )(page_table, qbd, key_minor(kf), key_minor(vf), key_minor(lf),
      *([ck] * group), *([cv] * group), *([cl] * group))
    o5 = o.reshape(nb, n_new, N_HEADS, N_HEADS, HEAD_DIM)
    return jnp.einsum('btggd->btgd', o5).reshape(nb * n_new, D_ATTN)


S5_COLS = 512


def _s5_prep_kernel(lre_ref, lim_ref, ldt_ref, bre_ref, bim_ref, jp_ref,
                    lbr_ref, lbi_ref, bbr_ref, bbi_ref, pwr_ref, pwi_ref):
    lre = lre_ref[...]
    lim = lim_ref[...]
    dt = jnp.exp(ldt_ref[...])
    mag = jnp.exp(lre * dt)
    ang = lim * dt
    lbr = mag * jnp.cos(ang)
    lbi = mag * jnp.sin(ang)
    lbr_ref[...] = lbr
    lbi_ref[...] = lbi
    nr = lbr - 1.0
    den = lre * lre + lim * lim
    cr = (nr * lre + lbi * lim) / den
    ci = (lbi * lre - nr * lim) / den
    bre = bre_ref[...]
    bim = bim_ref[...]
    bbr_ref[...] = cr * bre - ci * bim
    bbi_ref[...] = cr * bim + ci * bre
    jp = jp_ref[...]
    pmag = jnp.exp(jp * (lre * dt))
    pang = jp * ang
    pwr_ref[...] = pmag * jnp.cos(pang)
    pwi_ref[...] = pmag * jnp.sin(pang)


def _s5_prep(lam_re, lam_im, log_dt, b_re, b_im, chunk):
    lre = lam_re.reshape(1, N_STATE).astype(F32)
    lim = lam_im.reshape(1, N_STATE).astype(F32)
    ldt = jnp.repeat(log_dt.astype(F32), STATE_DIM).reshape(1, N_STATE)
    bre = jnp.transpose(b_re.astype(F32), (2, 0, 1)).reshape(SSM_GROUP, N_STATE)
    bim = jnp.transpose(b_im.astype(F32), (2, 0, 1)).reshape(SSM_GROUP, N_STATE)
    jp = (jnp.arange(chunk, dtype=jnp.int32) // SUBLANES + 1).astype(F32).reshape(chunk, 1)
    vec = jax.ShapeDtypeStruct((1, N_STATE), F32)
    mat = jax.ShapeDtypeStruct((SSM_GROUP, N_STATE), F32)
    tab = jax.ShapeDtypeStruct((chunk, N_STATE), F32)
    return pl.pallas_call(
        _s5_prep_kernel, name="s5_prep",
        out_shape=(vec, vec, mat, mat, tab, tab),
        compiler_params=pltpu.CompilerParams(vmem_limit_bytes=VMEM_LIMIT),
    )(lre, lim, ldt, bre, bim, jp)


def _s5_weights(bbr, bbi, c_re, c_im, d_skip, w_glu):
    eye = jnp.eye(N_GROUPS, dtype=F32)

    def in_map(bb):
        b3 = bb.reshape(SSM_GROUP, N_GROUPS, STATE_DIM)
        return jnp.einsum('cgp,gh->gchp', b3, eye).reshape(D_SSM, N_STATE).astype(BF16)

    def out_map(cc):
        return jnp.einsum('gcp,gh->gphc', cc.astype(F32), eye).reshape(N_STATE, D_SSM).astype(BF16)

    return (in_map(bbr), in_map(bbi), out_map(c_re), out_map(c_im),
            d_skip.astype(F32).reshape(1, D_SSM), w_glu.astype(BF16))


def _cmul_add(ar, ai, br, bi, cr, ci):
    return ar * br - ai * bi + cr, ar * bi + ai * br + ci


def _s5_body(u_ref, lbr_ref, lbi_ref, pwr_ref, pwi_ref, wbr_ref, wbi_ref, wcr_ref, wci_ref,
             dsk_ref, wg_ref, o_ref, hr_ref, hi_ref, tr_ref, ti_ref, start_states):
    chunk = u_ref.shape[1]
    seg = chunk // SUBLANES
    half_in = D_SSM // 2
    half_st = N_STATE // 2
    u = u_ref[0]
    ub = u.astype(BF16)
    for wref, href in ((wbr_ref, hr_ref), (wbi_ref, hi_ref)):
        href[:, :half_st] = jnp.dot(ub[:, :half_in], wref[:half_in, :half_st],
                                    preferred_element_type=F32)
        href[:, half_st:] = jnp.dot(ub[:, half_in:], wref[half_in:, half_st:],
                                    preferred_element_type=F32)

    ends_r, ends_i = [], []
    for cb in range(N_STATE // S5_COLS):
        cs = slice(cb * S5_COLS, (cb + 1) * S5_COLS)
        lr = jnp.broadcast_to(lbr_ref[:, cs], (SUBLANES, S5_COLS))
        li = jnp.broadcast_to(lbi_ref[:, cs], (SUBLANES, S5_COLS))

        def scan_step(j, carry):
            sr, si = carry
            rows = pl.ds(pl.multiple_of(j * SUBLANES, SUBLANES), SUBLANES)
            sr, si = _cmul_add(lr, li, sr, si, hr_ref[rows, cs], hi_ref[rows, cs])
            hr_ref[rows, cs] = sr
            hi_ref[rows, cs] = si
            return sr, si

        zero = jnp.zeros((SUBLANES, S5_COLS), F32)
        er, ei = lax.fori_loop(0, seg, scan_step, (zero, zero), unroll=min(seg, 4))
        ends_r.append(er)
        ends_i.append(ei)
    end_r = jnp.concatenate(ends_r, axis=1)
    end_i = jnp.concatenate(ends_i, axis=1)
    start_states(end_r, end_i)

    for cb in range(N_STATE // S5_COLS):
        cs = slice(cb * S5_COLS, (cb + 1) * S5_COLS)
        tr = tr_ref[:, cs]
        ti = ti_ref[:, cs]

        def fix_step(j, carry):
            rows = pl.ds(pl.multiple_of(j * SUBLANES, SUBLANES), SUBLANES)
            nr, ni = _cmul_add(pwr_ref[rows, cs], pwi_ref[rows, cs], tr, ti,
                               hr_ref[rows, cs], hi_ref[rows, cs])
            hr_ref[rows, cs] = nr
            hi_ref[rows, cs] = ni
            return carry

        lax.fori_loop(0, seg, fix_step, 0, unroll=min(seg, 4))

    half_out = D_SSM // 2
    ys = []
    for k in range(2):
        st = slice(k * half_st, (k + 1) * half_st)
        oc = slice(k * half_out, (k + 1) * half_out)
        ys.append(jnp.dot(hr_ref[:, st].astype(BF16), wcr_ref[st, oc], preferred_element_type=F32)
                  - jnp.dot(hi_ref[:, st].astype(BF16), wci_ref[st, oc], preferred_element_type=F32))
    y = jnp.concatenate(ys, axis=1) + dsk_ref[...] * u
    z = jax.nn.gelu(y, approximate=True)
    o_ref[0] = z * jax.nn.sigmoid(jnp.dot(z.astype(BF16), wg_ref[...], preferred_element_type=F32))


def _s5_seq_kernel(u_ref, lbr_ref, lbi_ref, pwr_ref, pwi_ref, wbr_ref, wbi_ref, wcr_ref, wci_ref,
                   dsk_ref, wg_ref, o_ref, hlr_ref, hli_ref,
                   hr_ref, hi_ref, tr_ref, ti_ref, cr_ref, ci_ref):
    chunk = u_ref.shape[1]

    @pl.when(pl.program_id(1) == 0)
    def _():
        cr_ref[...] = jnp.zeros_like(cr_ref)
        ci_ref[...] = jnp.zeros_like(ci_ref)

    def start_states(end_r, end_i):
        last = chunk - SUBLANES
        lsr = pwr_ref[last:last + 1, :]
        lsi = pwi_ref[last:last + 1, :]
        sr = cr_ref[...]
        si = ci_ref[...]
        for s in range(SUBLANES):
            tr_ref[s:s + 1, :] = sr
            ti_ref[s:s + 1, :] = si
            sr, si = _cmul_add(lsr, lsi, sr, si, end_r[s:s + 1, :], end_i[s:s + 1, :])
        cr_ref[...] = sr
        ci_ref[...] = si

    _s5_body(u_ref, lbr_ref, lbi_ref, pwr_ref, pwi_ref, wbr_ref, wbi_ref, wcr_ref, wci_ref,
             dsk_ref, wg_ref, o_ref, hr_ref, hi_ref, tr_ref, ti_ref, start_states)
    hlr_ref[0] = cr_ref[...]
    hli_ref[0] = ci_ref[...]


def _s5_batch_kernel(u_ref, h0r_ref, h0i_ref, lbr_ref, lbi_ref, pwr_ref, pwi_ref, wbr_ref, wbi_ref,
                     wcr_ref, wci_ref, dsk_ref, wg_ref, o_ref, hlr_ref, hli_ref,
                     hr_ref, hi_ref, tr_ref, ti_ref):
    chunk = u_ref.shape[1]

    def start_states(end_r, end_i):
        last = chunk - SUBLANES
        tr = h0r_ref[0]
        ti = h0i_ref[0]
        tr_ref[...] = tr
        ti_ref[...] = ti
        fr, fi = _cmul_add(pwr_ref[last:last + SUBLANES, :], pwi_ref[last:last + SUBLANES, :],
                           tr, ti, end_r, end_i)
        hlr_ref[0] = fr
        hli_ref[0] = fi

    _s5_body(u_ref, lbr_ref, lbi_ref, pwr_ref, pwi_ref, wbr_ref, wbi_ref, wcr_ref, wci_ref,
             dsk_ref, wg_ref, o_ref, hr_ref, hi_ref, tr_ref, ti_ref, start_states)


def _to_segment_rows(u, chunk):
    *lead, s, d = u.shape
    seg = chunk // SUBLANES
    x = u.reshape(*lead, s // chunk, SUBLANES, seg, d)
    return jnp.swapaxes(x, -2, -3).reshape(*lead, s, d)


def _from_segment_rows(u, chunk):
    *lead, s, d = u.shape
    seg = chunk // SUBLANES
    x = u.reshape(*lead, s // chunk, seg, SUBLANES, d)
    return jnp.swapaxes(x, -2, -3).reshape(*lead, s, d)


def _s5_scratch(chunk):
    return [pltpu.VMEM((chunk, N_STATE), F32), pltpu.VMEM((chunk, N_STATE), F32),
            pltpu.VMEM((SUBLANES, N_STATE), F32), pltpu.VMEM((SUBLANES, N_STATE), F32)]


def _s5_seq(u, prep, weights, chunk):
    n, s, d = u.shape
    lbr, lbi, _, _, pwr, pwi = prep
    consts = (lbr, lbi, pwr, pwi) + tuple(weights)
    state = jax.ShapeDtypeStruct((n, 1, N_STATE), F32)
    return pl.pallas_call(
        _s5_seq_kernel, name="s5_seq",
        grid=(n, s // chunk),
        in_specs=[pl.BlockSpec((1, chunk, d), lambda b, i: (b, i, 0))]
                 + [_const_spec(a.shape) for a in consts],
        out_specs=(pl.BlockSpec((1, chunk, d), lambda b, i: (b, i, 0)),
                   pl.BlockSpec((1, 1, N_STATE), lambda b, i: (b, 0, 0)),
                   pl.BlockSpec((1, 1, N_STATE), lambda b, i: (b, 0, 0))),
        out_shape=(jax.ShapeDtypeStruct((n, s, d), F32), state, state),
        scratch_shapes=_s5_scratch(chunk) + [pltpu.VMEM((1, N_STATE), F32), pltpu.VMEM((1, N_STATE), F32)],
        compiler_params=_cparams(("arbitrary", "arbitrary")),
    )(u, *consts)


def _s5_batch(u, h0r, h0i, prep, weights):
    nblk, chunk, d = u.shape
    lbr, lbi, _, _, pwr, pwi = prep
    consts = (lbr, lbi, pwr, pwi) + tuple(weights)
    state = jax.ShapeDtypeStruct((nblk, SUBLANES, N_STATE), F32)
    blk = lambda shape: pl.BlockSpec(shape, lambda i: (i, 0, 0))
    return pl.pallas_call(
        _s5_batch_kernel, name="s5_batch",
        grid=(nblk,),
        in_specs=[blk((1, chunk, d)), blk((1, SUBLANES, N_STATE)), blk((1, SUBLANES, N_STATE))]
                 + [_const_spec(a.shape) for a in consts],
        out_specs=(blk((1, chunk, d)), blk((1, SUBLANES, N_STATE)), blk((1, SUBLANES, N_STATE))),
        out_shape=(jax.ShapeDtypeStruct((nblk, chunk, d), F32), state, state),
        scratch_shapes=_s5_scratch(chunk),
        compiler_params=_cparams(("arbitrary",)),
    )(u, h0r, h0i, *consts)


_PEER_CELLS = [(i, j) for i in range(PEER_TOPK) for j in range(PEER_TOPK)
               if (i + 1) * (j + 1) <= PEER_TOPK]
_N_CELLS_PAD = -(-len(_PEER_CELLS) // SUBLANES) * SUBLANES
_BIG_ID = 1 << 20


def _pop_max(vals, ids):
    level = list(zip(vals, ids))
    while len(level) > 1:
        nxt = []
        for k in range(0, len(level) - 1, 2):
            (a, ia), (b, ib) = level[k], level[k + 1]
            take = a >= b
            nxt.append((jnp.where(take, a, b), jnp.where(take, ia, ib)))
        if len(level) % 2:
            nxt.append(level[-1])
        level = nxt
    v8, i8 = level[0]
    m = jnp.max(v8, axis=0, keepdims=True)
    idx = jnp.min(jnp.where(v8 == m, i8, _BIG_ID), axis=0, keepdims=True)
    return m, idx, [jnp.where(i == idx, NEG_INF, v) for v, i in zip(vals, ids)]


def _peer_route_kernel(xn_ref, wq_ref, keys_ref, cell_ref, a_ref, b_ref, w_ref,
                       q_ref, sv_ref, si_ref, cand_ref, cv_ref, fid_ref, ao_ref, bo_ref, wo_ref):
    t = xn_ref.shape[0]
    q_ref[...] = jnp.dot(xn_ref[...], wq_ref[...], preferred_element_type=F32)
    sub = lax.broadcasted_iota(jnp.int32, (SUBLANES, t), 0)
    key_ids = [sub + v * SUBLANES for v in range(N_KEYS // SUBLANES)]

    def slabs(x):
        return [x[v * SUBLANES:(v + 1) * SUBLANES, :] for v in range(x.shape[0] // SUBLANES)]

    def sub_topk(ht, carry):
        qh = q_ref[:, pl.ds(pl.multiple_of(ht * D_HALF, D_HALF), D_HALF)].astype(BF16)
        s = slabs(lax.dot_general(keys_ref[ht], qh, (((1,), (1,)), ((), ())),
                                  preferred_element_type=F32))
        for r in range(PEER_TOPK):
            m, idx, s = _pop_max(s, key_ids)
            sv_ref[ht, r:r + 1, :] = m
            si_ref[ht, r:r + 1, :] = idx
        return carry

    lax.fori_loop(0, 2 * PEER_HEADS, sub_topk, 0)

    cell_ids = slabs(cell_ref[...])

    def combine(h, carry):
        va = sv_ref[2 * h]
        vb = sv_ref[2 * h + 1]
        cand_ref[...] = jnp.full((_N_CELLS_PAD, t), NEG_INF, F32)
        for c, (i, j) in enumerate(_PEER_CELLS):
            cand_ref[c:c + 1, :] = va[i:i + 1, :] + vb[j:j + 1, :]
        cand = slabs(cand_ref[...])
        for r in range(PEER_TOPK):
            m, fid, cand = _pop_max(cand, cell_ids)
            cv_ref[r:r + 1, :] = m
            fid_ref[r:r + 1, :] = fid
        cv = cv_ref[...]
        fid = fid_ref[...]
        isel = lax.shift_right_logical(fid, 4)
        jsel = jnp.bitwise_and(fid, PEER_TOPK - 1)
        ia = si_ref[2 * h]
        ib = si_ref[2 * h + 1]
        ka = jnp.zeros((PEER_TOPK, t), jnp.int32)
        kb = jnp.zeros((PEER_TOPK, t), jnp.int32)
        for r in range(PEER_TOPK):
            ka = jnp.where(isel == r, ia[r:r + 1, :], ka)
            kb = jnp.where(jsel == r, ib[r:r + 1, :], kb)
        e = jnp.exp(cv - cv[0:1, :])
        ao_ref[h] = ka
        bo_ref[h] = kb
        wo_ref[h] = e / jnp.sum(e, axis=0, keepdims=True)
        return carry

    lax.fori_loop(0, PEER_HEADS, combine, 0)
    nk = PEER_HEADS * PEER_TOPK
    a_ref[...] = ao_ref[...].reshape(nk, t).T
    b_ref[...] = bo_ref[...].reshape(nk, t).T
    w_ref[...] = wo_ref[...].reshape(nk, t).T


def _peer_route(xn, wq, keys, t):
    n, d = xn.shape
    nk = PEER_HEADS * PEER_TOPK
    ids = [i * PEER_TOPK + j for i, j in _PEER_CELLS] + [_BIG_ID] * (_N_CELLS_PAD - len(_PEER_CELLS))
    cell = jnp.broadcast_to(jnp.asarray(ids, jnp.int32)[:, None], (_N_CELLS_PAD, t))
    tok = pl.BlockSpec((t, nk), lambda i: (i, 0))
    return pl.pallas_call(
        _peer_route_kernel, name="peer_route",
        grid=(n // t,),
        in_specs=[pl.BlockSpec((t, d), lambda i: (i, 0)), _const_spec(wq.shape),
                  _const_spec(keys.shape), _const_spec(cell.shape)],
        out_specs=(tok, tok, tok),
        out_shape=(jax.ShapeDtypeStruct((n, nk), jnp.int32), jax.ShapeDtypeStruct((n, nk), jnp.int32),
                   jax.ShapeDtypeStruct((n, nk), F32)),
        scratch_shapes=[pltpu.VMEM((t, wq.shape[1]), F32),
                        pltpu.VMEM((2 * PEER_HEADS, PEER_TOPK, t), F32),
                        pltpu.VMEM((2 * PEER_HEADS, PEER_TOPK, t), jnp.int32),
                        pltpu.VMEM((_N_CELLS_PAD, t), F32),
                        pltpu.VMEM((PEER_TOPK, t), F32), pltpu.VMEM((PEER_TOPK, t), jnp.int32),
                        pltpu.VMEM((PEER_HEADS, PEER_TOPK, t), jnp.int32),
                        pltpu.VMEM((PEER_HEADS, PEER_TOPK, t), jnp.int32),
                        pltpu.VMEM((PEER_HEADS, PEER_TOPK, t), F32)],
        compiler_params=_cparams(("arbitrary",)),
    )(xn, wq, keys, cell)


EXPERT_BLOCK = 2048
HIDDEN_BLOCK = 4096
COEF_PITCH = 72
HALF_GROUPS = N_KEYS // 2


def _peer_hidden_kernel(xn_ref, a_ref, b_ref, w_ref, u_ref, c_ref, hsel_ref):
    j = pl.program_id(1)
    groups = u_ref.shape[0] // N_KEYS

    @pl.when(j == 0)
    def _():
        hsel_ref[...] = jnp.zeros_like(hsel_ref)

    h = lax.dot_general(xn_ref[...], u_ref[...], (((1,), (1,)), ((), ())),
                        preferred_element_type=F32)
    a_idx = a_ref[...]
    b_idx = b_ref[...]
    hs = hsel_ref[...]
    for g in range(groups):
        hg = h[:, g * N_KEYS:(g + 1) * N_KEYS]
        hs = jnp.where(a_idx == j * groups + g, jnp.take_along_axis(hg, b_idx, axis=1), hs)
    hsel_ref[...] = hs

    @pl.when(j == pl.num_programs(1) - 1)
    def _():
        c_ref[...] = w_ref[...] * jax.nn.gelu(hs, approximate=True)


def _peer_hidden(xn, a_idx, b_idx, w, u_bf16, t):
    n, d = xn.shape
    nk = a_idx.shape[1]
    ne = u_bf16.shape[0]
    tok = lambda width: pl.BlockSpec((t, width), lambda i, j: (i, 0))
    return pl.pallas_call(
        _peer_hidden_kernel, name="peer_hidden",
        grid=(n // t, ne // HIDDEN_BLOCK),
        in_specs=[tok(d), tok(nk), tok(nk), tok(nk),
                  pl.BlockSpec((HIDDEN_BLOCK, d), lambda i, j: (j, 0))],
        out_specs=tok(nk),
        out_shape=jax.ShapeDtypeStruct((n, nk), F32),
        scratch_shapes=[pltpu.VMEM((t, nk), F32)],
        compiler_params=_cparams(("arbitrary", "arbitrary")),
    )(xn, a_idx, b_idx, w, u_bf16)


def _peer_out_kernel(a_ref, b_ref, c_ref, x1_ref, v_ref, y_ref, s_ref, acc_ref):
    j = pl.program_id(1)
    t = a_ref.shape[0]
    pairs = v_ref.shape[0] // (2 * N_KEYS)
    hi_mask = jnp.uint32(0xFFFF0000)

    @pl.when(j == 0)
    def _():
        sub = lax.broadcasted_iota(jnp.int32, (N_KEYS, N_KEYS), 0)

        def one_token(i, carry):
            arow = a_ref[pl.ds(i, 1), :]
            brow = b_ref[pl.ds(i, 1), :]
            crow = c_ref[pl.ds(i, 1), :]
            pt = jnp.where(sub == arow, 1.0, 0.0).astype(BF16)
            rt = jnp.where(sub == brow, crow, 0.0).astype(BF16)
            coef = lax.dot_general(pt, rt, (((1,), (1,)), ((), ())), preferred_element_type=F32)
            bits = pltpu.bitcast(coef.astype(BF16).astype(F32), jnp.uint32)
            packed = jnp.bitwise_or(jnp.bitwise_and(bits[:HALF_GROUPS], hi_mask),
                                    lax.shift_right_logical(bits[HALF_GROUPS:], jnp.uint32(16)))
            s_ref[pl.ds(pl.multiple_of(i * COEF_PITCH, SUBLANES), HALF_GROUPS), :] = packed
            return carry

        lax.fori_loop(0, t, one_token, 0, unroll=32)

    parts = []
    for g in range(pairs):
        w32 = s_ref[pl.ds(j * pairs + g, t, stride=COEF_PITCH), :]
        parts.append(pltpu.bitcast(jnp.bitwise_and(w32, hi_mask), F32).astype(BF16))
        parts.append(pltpu.bitcast(lax.shift_left(w32, jnp.uint32(16)), F32).astype(BF16))
    coef_blk = jnp.concatenate(parts, axis=1)
    contrib = jnp.dot(coef_blk, v_ref[...], preferred_element_type=F32)

    @pl.when(j == 0)
    def _():
        acc_ref[...] = contrib

    @pl.when(j > 0)
    def _():
        acc_ref[...] += contrib

    @pl.when(j == pl.num_programs(1) - 1)
    def _():
        y_ref[...] = x1_ref[...] + acc_ref[...]


def _pair_groups(table_bf16):
    ne, d = table_bf16.shape
    x = table_bf16.reshape(2, HALF_GROUPS, N_KEYS, d)
    return jnp.swapaxes(x, 0, 1).reshape(ne, d)


def _peer_out(a_idx, b_idx, c, x1, v_paired, t):
    n, d = x1.shape
    nk = a_idx.shape[1]
    ne = v_paired.shape[0]
    tok = lambda width: pl.BlockSpec((t, width), lambda i, j: (i, 0))
    return pl.pallas_call(
        _peer_out_kernel, name="peer_out",
        grid=(n // t, ne // EXPERT_BLOCK),
        in_specs=[tok(nk), tok(nk), tok(nk), tok(d),
                  pl.BlockSpec((EXPERT_BLOCK, d), lambda i, j: (j, 0))],
        out_specs=tok(d),
        out_shape=jax.ShapeDtypeStruct((n, d), F32),
        scratch_shapes=[pltpu.VMEM((t * COEF_PITCH, N_KEYS), jnp.uint32), pltpu.VMEM((t, d), F32)],
        compiler_params=_cparams(("arbitrary", "arbitrary")),
    )(a_idx, b_idx, c, x1, v_paired)


TOKEN_TILE = 512
ATTN_TILE = 1024
S5_CHUNK = 256
ROUTE_TILE = 512
EXPERT_TILE = 512


def _tile(n, pref):
    t = min(pref, n)
    while n % t:
        t //= 2
    return t


def _tail(x1, xn2, lw):
    n = x1.shape[0]
    a_idx, b_idx, w = _peer_route(xn2, lw['wq'], lw['keys'], _tile(n, ROUTE_TILE))
    te = _tile(n, EXPERT_TILE)
    c = _peer_hidden(xn2, a_idx, b_idx, w, lw['u'], te)
    return _peer_out(a_idx, b_idx, c, x1, lw['v'], te)


def _layer_weights(l, norm1_g, w_in, b_forget, q_norm_g, k_norm_g, ssm_lam_re, ssm_lam_im, ssm_log_dt,
                   ssm_b_re, ssm_b_im, ssm_c_re, ssm_c_im, ssm_d, w_glu, w_proj_attn, w_proj_ssm, w_out,
                   norm2_g, w_query, sub_keys, expert_u, expert_v, chunks):
    lw = {'inproj': _prep_inproj(norm1_g[l], w_in[l], b_forget[l], q_norm_g[l], k_norm_g[l])}
    lw['s5_prep'] = {c: _s5_prep(ssm_lam_re[l], ssm_lam_im[l], ssm_log_dt[l], ssm_b_re[l], ssm_b_im[l], c)
                     for c in chunks}
    p0 = lw['s5_prep'][chunks[0]]
    lw['s5_w'] = _s5_weights(p0[2], p0[3], ssm_c_re[l], ssm_c_im[l], ssm_d[l], w_glu[l])
    lw['mix'] = (w_proj_attn[l].astype(BF16), w_proj_ssm[l].astype(BF16), w_out[l].astype(BF16),
                 norm2_g[l].astype(F32)[None, :])
    lw['wq'] = w_query[l].astype(BF16)
    lw['keys'] = sub_keys[l].reshape(2 * PEER_HEADS, N_KEYS, D_HALF).astype(BF16)
    lw['u'] = expert_u[l].astype(BF16)
    lw['v'] = _pair_groups(expert_v[l].astype(BF16))
    return lw


def kernel(x_prompt, x_sample, cache_k, cache_v, cache_logf, state_ssm_re, state_ssm_im, page_table,
           norm1_g, w_in, b_forget, q_norm_g, k_norm_g,
           ssm_lam_re, ssm_lam_im, ssm_log_dt, ssm_b_re, ssm_b_im, ssm_c_re, ssm_c_im, ssm_d, w_glu,
           w_proj_attn, w_proj_ssm, w_out, norm2_g, w_query, sub_keys, expert_u, expert_v):
    nb, s, d = x_prompt.shape
    db, t, _ = x_sample.shape
    depth = w_in.shape[0]
    assert db % SUBLANES == 0 and t * N_HEADS <= LANES and t <= PAGE
    batch_chunk = SUBLANES * t
    y_p = x_prompt.reshape(nb * s, d)
    y_s = x_sample.reshape(db * t, d)
    outs = [[] for _ in range(10)]
    for l in range(depth):
        lw = _layer_weights(l, norm1_g, w_in, b_forget, q_norm_g, k_norm_g, ssm_lam_re, ssm_lam_im,
                            ssm_log_dt, ssm_b_re, ssm_b_im, ssm_c_re, ssm_c_im, ssm_d, w_glu,
                            w_proj_attn, w_proj_ssm, w_out, norm2_g, w_query, sub_keys, expert_u,
                            expert_v, (S5_CHUNK, batch_chunk))

        tm = _tile(nb * s, TOKEN_TILE)
        qb, kb, vb, kf, vf, lf, u, ga, gs = _inproj(y_p, *lw['inproj'], tm=tm, seq_len=s)
        token_major = lambda x: jnp.transpose(x.reshape(nb, N_HEADS, HEAD_DIM, s), (0, 3, 1, 2))
        lf3 = lf.reshape(nb, s, N_HEADS)
        ta = _tile(s, ATTN_TILE)
        ct = _cumsum(jnp.swapaxes(lf3, 1, 2), ta)
        o_attn = _attn_prompt(qb.reshape(nb, s, D_ATTN), kb.reshape(nb, s, D_ATTN),
                              vb.reshape(nb, s, D_ATTN), ct, ta)
        u_seg = _to_segment_rows(u.reshape(nb, s, D_SSM), S5_CHUNK)
        o_ssm, hr, hi = _s5_seq(u_seg, lw['s5_prep'][S5_CHUNK], lw['s5_w'], S5_CHUNK)
        o_ssm = _from_segment_rows(o_ssm, S5_CHUNK).reshape(nb * s, D_SSM)
        x1, xn2 = _mix(o_attn.reshape(nb * s, D_ATTN), o_ssm, ga, gs, y_p, *lw['mix'], tm=tm)
        y_p = _tail(x1, xn2, lw)
        for slot, val in zip(range(5), (token_major(kf), token_major(vf), lf3,
                                        hr.reshape(nb, N_GROUPS, STATE_DIM),
                                        hi.reshape(nb, N_GROUPS, STATE_DIM))):
            outs[slot].append(val)

        tm = _tile(db * t, TOKEN_TILE)
        qb, kb, vb, kf, vf, lf, u, ga, gs = _inproj(y_s, *lw['inproj'], tm=tm)
        o_attn = _attn_paged(page_table, qb, kf, vf, lf, cache_k, cache_v, cache_logf, t, l).astype(BF16)
        nblk = db // SUBLANES
        u_seg = _to_segment_rows(u.reshape(nblk, batch_chunk, D_SSM), batch_chunk)
        h0r = state_ssm_re[l].astype(F32).reshape(nblk, SUBLANES, N_STATE)
        h0i = state_ssm_im[l].astype(F32).reshape(nblk, SUBLANES, N_STATE)
        o_ssm, hr, hi = _s5_batch(u_seg, h0r, h0i, lw['s5_prep'][batch_chunk], lw['s5_w'])
        o_ssm = _from_segment_rows(o_ssm, batch_chunk).reshape(db * t, D_SSM)
        x1, xn2 = _mix(o_attn, o_ssm, ga, gs, y_s, *lw['mix'], tm=tm)
        y_s = _tail(x1, xn2, lw)
        for slot, val in zip(range(5, 10), (kf.reshape(db, t, N_HEADS, HEAD_DIM),
                                            vf.reshape(db, t, N_HEADS, HEAD_DIM),
                                            lf.reshape(db, t, N_HEADS),
                                            hr.reshape(db, N_GROUPS, STATE_DIM),
                                            hi.reshape(db, N_GROUPS, STATE_DIM))):
            outs[slot].append(val)
    stacked = [jnp.stack(o) for o in outs]
    return (y_p.reshape(nb, s, d), y_s.reshape(db, t, d), *stacked)
```
